```python
import jax, jax.numpy as jnp
from jax import lax
import numpy as np

D_MODEL = 1024
BATCH = 2
SEQ = 16384
DEPTH = 4
DEC_BATCH = 1
DEC_SEQ = 16384
PAST_LEN = 128

GRID_W = 64
EPS = 1e-6
N_EVEN = (DEPTH + 1) // 2
N_ODD = DEPTH // 2

GLA_HEADS = 4
GLA_DK = D_MODEL // (2 * GLA_HEADS)
GLA_DV = D_MODEL // GLA_HEADS
GLA_LOWRANK = 16
GLA_TAU = 16.0
GLA_CHUNK = 64
ATT_HEADS = 8
ATT_KV_HEADS = 2
ATT_HD = 128
ATT_GROUP = ATT_HEADS // ATT_KV_HEADS
ATT_BLOCK = 128
ROPE_THETA = 10000.0
POOL_WINDOWS = (2, 4, 8, 16)
POOL_GROUPS = 4
POOL_WIDTH = 2 * D_MODEL
POOL_GW = POOL_WIDTH // POOL_GROUPS

A_WIDTH = GLA_HEADS * GLA_DV
B_WIDTH = ATT_HEADS * ATT_HD
EV_SPLITS = (GLA_HEADS * GLA_DK, GLA_HEADS * GLA_DK, A_WIDTH, GLA_LOWRANK, GLA_LOWRANK, A_WIDTH,
             B_WIDTH, ATT_KV_HEADS * ATT_HD, ATT_KV_HEADS * ATT_HD, B_WIDTH)
EV_IN = 2 * GLA_HEADS * GLA_DK + 2 * A_WIDTH + 2 * GLA_LOWRANK + 2 * B_WIDTH + 2 * ATT_KV_HEADS * ATT_HD
OD_IN = 2 * POOL_WIDTH

kernel_name = "hybrid_gla_gqa2drope_pool_encoder"


def rmsnorm(x, g):
    xf = x.astype(jnp.float32)
    y = xf * lax.rsqrt(jnp.mean(xf * xf, axis=-1, keepdims=True) + EPS)
    return (y * g.astype(jnp.float32)).astype(x.dtype)


def ev_offsets():
    return [int(o) for o in np.cumsum(np.array(EV_SPLITS))[:-1]]


def gla_causal(q, k, v, log_a):
    B, S, H, DK = q.shape
    DV = v.shape[-1]
    C = GLA_CHUNK
    nc = S // C
    qc = q.astype(jnp.float32).reshape(B, nc, C, H, DK)
    kc = k.astype(jnp.float32).reshape(B, nc, C, H, DK)
    vc = v.astype(jnp.float32).reshape(B, nc, C, H, DV)
    b = jnp.cumsum(log_a.reshape(B, nc, C, H, DK), axis=2)
    b_last = b[:, :, -1]
    q_dec = qc * jnp.exp(b)
    k_inv = kc * jnp.exp(-b)
    k_end = kc * jnp.exp(b_last[:, :, None] - b)
    mask = jnp.tril(jnp.ones((C, C), dtype=bool))
    scores = jnp.where(mask, jnp.einsum('bnihd,bnjhd->bnhij', q_dec, k_inv), 0.0)
    o_intra = jnp.einsum('bnhij,bnjhv->bnihv', scores, vc)

    def step(state, xs):
        q_n, k_n, v_n, dec_n = xs
        o_n = jnp.einsum('bihd,bhdv->bihv', q_n, state)
        state = state * jnp.exp(dec_n)[..., None] + jnp.einsum('bjhd,bjhv->bhdv', k_n, v_n)
        return state, o_n

    state0 = jnp.zeros((B, H, DK, DV), jnp.float32)
    xs = (jnp.moveaxis(q_dec, 1, 0), jnp.moveaxis(k_end, 1, 0),
          jnp.moveaxis(vc, 1, 0), jnp.moveaxis(b_last, 1, 0))
    _, o_inter = lax.scan(step, state0, xs)
    o = o_intra + jnp.moveaxis(o_inter, 0, 1)
    return o.reshape(B, S, H, DV)


def gla_bidirectional(q, k, v, la_f, la_b):
    flip = lambda t: jnp.flip(t, axis=1)
    fwd = gla_causal(q, k, v, la_f)
    bwd = flip(gla_causal(flip(q), flip(k), flip(v), flip(la_b)))
    return fwd + bwd


def axial_rope_tables(S):
    rows = S // GRID_W
    row = jnp.broadcast_to(jnp.arange(rows, dtype=jnp.float32)[:, None], (rows, GRID_W)).reshape(S)
    col = jnp.broadcast_to(jnp.arange(GRID_W, dtype=jnp.float32)[None, :], (rows, GRID_W)).reshape(S)
    n_pairs = ATT_HD // 4
    freqs = ROPE_THETA ** (-jnp.arange(n_pairs, dtype=jnp.float32) / n_pairs)
    ang = jnp.concatenate([row[:, None] * freqs, col[:, None] * freqs], axis=-1)
    return jnp.cos(ang), jnp.sin(ang)


def apply_rope(x, cos, sin):
    B, S, H, HD = x.shape
    xf = x.astype(jnp.float32).reshape(B, S, H, HD // 2, 2)
    x0, x1 = xf[..., 0], xf[..., 1]
    c = cos[None, :, None, :]
    s = sin[None, :, None, :]
    out = jnp.stack([x0 * c - x1 * s, x0 * s + x1 * c], axis=-1)
    return out.reshape(B, S, H, HD).astype(x.dtype)


def block_attention(q, k, v):
    B, S, HKV, G, HD = q.shape
    nb = S // ATT_BLOCK
    qb = jnp.moveaxis(q.reshape(B, nb, ATT_BLOCK, HKV, G, HD), 1, 0)
    scale = HD ** -0.5

    def one_block(qblk):
        s = jnp.einsum('bqkgd,bskd->bkgqs', qblk, k, preferred_element_type=jnp.float32) * scale
        p = jax.nn.softmax(s, axis=-1)
        return jnp.einsum('bkgqs,bskd->bqkgd', p.astype(v.dtype), v)

    out = lax.map(one_block, qb)
    return jnp.moveaxis(out, 0, 1).reshape(B, S, HKV * G * HD)


def multiscale_pool_residual(u):
    B, S, _ = u.shape
    uf = u.astype(jnp.float32).reshape(B, S, POOL_GROUPS, POOL_GW)
    cs = jnp.concatenate([jnp.zeros_like(uf[:, :1]), jnp.cumsum(uf, axis=1)], axis=1)
    t = jnp.arange(S)
    outs = []
    for g, w in enumerate(POOL_WINDOWS):
        lo = jnp.clip(t - w // 2, 0, S)
        hi = jnp.clip(t - w // 2 + w, 0, S)
        csg = cs[:, :, g]
        window_sum = jnp.take(csg, hi, axis=1) - jnp.take(csg, lo, axis=1)
        cnt = (hi - lo).astype(jnp.float32)
        outs.append(window_sum / cnt[None, :, None])
    pooled = jnp.stack(outs, axis=2)
    return pooled - uf


def even_layer(x, norm_g, w_in, gate_w, gate_b, gla_norm_g, q_norm_g, k_norm_g, w_out, cos, sin):
    B, S, _ = x.shape
    h = rmsnorm(x, norm_g)
    proj = jnp.einsum('bsd,de->bse', h, w_in)
    q_a, k_a, v_a, lr_f, lr_b, z_a, q_b, k_b, v_b, z_b = jnp.split(proj, ev_offsets(), axis=-1)
    q_a = q_a.reshape(B, S, GLA_HEADS, GLA_DK) * (GLA_DK ** -0.5)
    k_a = k_a.reshape(B, S, GLA_HEADS, GLA_DK)
    v_a = v_a.reshape(B, S, GLA_HEADS, GLA_DV)
    la_f = jax.nn.log_sigmoid((jnp.einsum('bsr,re->bse', lr_f, gate_w[0]) + gate_b[0]).astype(jnp.float32)) / GLA_TAU
    la_b = jax.nn.log_sigmoid((jnp.einsum('bsr,re->bse', lr_b, gate_w[1]) + gate_b[1]).astype(jnp.float32)) / GLA_TAU
    la_f = la_f.reshape(B, S, GLA_HEADS, GLA_DK)
    la_b = la_b.reshape(B, S, GLA_HEADS, GLA_DK)
    o_a = gla_bidirectional(q_a, k_a, v_a, la_f, la_b).astype(x.dtype)
    o_a = rmsnorm(o_a, gla_norm_g.reshape(GLA_HEADS, GLA_DV)).reshape(B, S, A_WIDTH)
    o_a = o_a * jax.nn.silu(z_a)
    q_b = apply_rope(rmsnorm(q_b.reshape(B, S, ATT_HEADS, ATT_HD), q_norm_g), cos, sin)
    k_b = apply_rope(rmsnorm(k_b.reshape(B, S, ATT_KV_HEADS, ATT_HD), k_norm_g), cos, sin)
    v_b = v_b.reshape(B, S, ATT_KV_HEADS, ATT_HD)
    o_b = block_attention(q_b.reshape(B, S, ATT_KV_HEADS, ATT_GROUP, ATT_HD), k_b, v_b)
    o_b = o_b * jax.nn.silu(z_b)
    merged = jnp.concatenate([o_a, o_b], axis=-1)
    return jnp.einsum('bse,ed->bsd', merged, w_out)


def odd_layer(x, norm_g, w_in, pool_w, pool_b, pool_scale, w_out):
    B, S, _ = x.shape
    h = rmsnorm(x, norm_g)
    proj = jnp.einsum('bsd,de->bse', h, w_in)
    u, z = jnp.split(proj, 2, axis=-1)
    d = multiscale_pool_residual(u).astype(x.dtype)
    m = jnp.einsum('bsgc,gce->bsge', d, pool_w).reshape(B, S, POOL_WIDTH) + pool_b
    m = m * pool_scale * jax.nn.silu(z)
    return jnp.einsum('bse,ed->bsd', m, w_out)


def trunk(x, norm_g, final_norm_g, ev_w_in, ev_gla_gate_w, ev_gla_gate_b, ev_gla_norm_g,
          ev_q_norm_g, ev_k_norm_g, ev_w_out, od_w_in, od_pool_w, od_pool_b, od_pool_scale, od_w_out):
    S = x.shape[1]
    cos, sin = axial_rope_tables(S)
    for layer in range(DEPTH):
        i = layer // 2
        if layer % 2 == 0:
            x = x + even_layer(x, norm_g[layer], ev_w_in[i], ev_gla_gate_w[i], ev_gla_gate_b[i],
                               ev_gla_norm_g[i], ev_q_norm_g[i], ev_k_norm_g[i], ev_w_out[i], cos, sin)
        else:
            x = x + odd_layer(x, norm_g[layer], od_w_in[i], od_pool_w[i], od_pool_b[i],
                              od_pool_scale[i], od_w_out[i])
    return rmsnorm(x, final_norm_g)


def setup_inputs(seed: int = 0) -> dict:
    key = jax.random.key(seed)
    ks = jax.random.split(key, 16)
    n = lambda k, shape: jax.random.normal(k, shape, jnp.float32)
    return {
        "x_prompt": n(ks[0], (BATCH, SEQ, D_MODEL)),
        "x_sample": n(ks[1], (DEC_BATCH, DEC_SEQ, D_MODEL)),
        "norm_g": 1.0 + 0.02 * n(ks[2], (DEPTH, D_MODEL)),
        "final_norm_g": 1.0 + 0.02 * n(ks[3], (D_MODEL,)),
        "ev_w_in": n(ks[4], (N_EVEN, D_MODEL, EV_IN)) * D_MODEL ** -0.5,
        "ev_gla_gate_w": n(ks[5], (N_EVEN, 2, GLA_LOWRANK, GLA_HEADS * GLA_DK)) * GLA_LOWRANK ** -0.5,
        "ev_gla_gate_b": 0.1 * n(ks[6], (N_EVEN, 2, GLA_HEADS * GLA_DK)),
        "ev_gla_norm_g": 1.0 + 0.02 * n(ks[7], (N_EVEN, A_WIDTH)),
        "ev_q_norm_g": 1.0 + 0.02 * n(ks[8], (N_EVEN, ATT_HD)),
        "ev_k_norm_g": 1.0 + 0.02 * n(ks[9], (N_EVEN, ATT_HD)),
        "ev_w_out": n(ks[10], (N_EVEN, A_WIDTH + B_WIDTH, D_MODEL)) * (A_WIDTH + B_WIDTH) ** -0.5,
        "od_w_in": n(ks[11], (N_ODD, D_MODEL, OD_IN)) * D_MODEL ** -0.5,
        "od_pool_w": n(ks[12], (N_ODD, POOL_GROUPS, POOL_GW, POOL_GW)) * POOL_GW ** -0.5,
        "od_pool_b": 0.02 * n(ks[13], (N_ODD, POOL_WIDTH)),
        "od_pool_scale": 1.0 + 0.1 * n(ks[14], (N_ODD, POOL_WIDTH)),
        "od_w_out": n(ks[15], (N_ODD, POOL_WIDTH, D_MODEL)) * POOL_WIDTH ** -0.5,
    }


def reference(x_prompt, x_sample, norm_g, final_norm_g, ev_w_in, ev_gla_gate_w, ev_gla_gate_b,
              ev_gla_norm_g, ev_q_norm_g, ev_k_norm_g, ev_w_out, od_w_in, od_pool_w, od_pool_b,
              od_pool_scale, od_w_out):
    y_prompt = trunk(x_prompt, norm_g, final_norm_g, ev_w_in, ev_gla_gate_w, ev_gla_gate_b,
                     ev_gla_norm_g, ev_q_norm_g, ev_k_norm_g, ev_w_out, od_w_in, od_pool_w,
                     od_pool_b, od_pool_scale, od_w_out)
    y_sample = trunk(x_sample, norm_g, final_norm_g, ev_w_in, ev_gla_gate_w, ev_gla_gate_b,
                     ev_gla_norm_g, ev_q_norm_g, ev_k_norm_g, ev_w_out, od_w_in, od_pool_w,
                     od_pool_b, od_pool_scale, od_w_out)
    return (y_prompt, y_sample)
```

```python
import functools

import numpy as np
import jax
import jax.numpy as jnp
from jax import lax
from jax.experimental import pallas as pl
from jax.experimental.pallas import tpu as pltpu

F32 = jnp.float32
BF16 = jnp.bfloat16

D_MODEL = 1024
DEPTH = 4
GRID_W = 64
EPS = 1e-6

GLA_HEADS = 4
GLA_DK = 128
GLA_DV = 256
GLA_LOWRANK = 16
GLA_TAU = 16.0
GLA_CHUNK = 64
GLA_SUPER = 256

ATT_HEADS = 8
ATT_KV_HEADS = 2
ATT_HD = 128
ATT_GROUP = ATT_HEADS // ATT_KV_HEADS
ROPE_THETA = 10000.0
VT_ROWS = ATT_HD + 16

POOL_WINDOWS = (2, 4, 8, 16)
POOL_GROUPS = 4
POOL_WIDTH = 2 * D_MODEL
POOL_GW = POOL_WIDTH // POOL_GROUPS
POOL_HALO = 8

A_WIDTH = GLA_HEADS * GLA_DV
B_WIDTH = ATT_HEADS * ATT_HD
QK_A = GLA_HEADS * GLA_DK
KV_B = ATT_KV_HEADS * ATT_HD
EV_SPLITS = (QK_A, QK_A, A_WIDTH, GLA_LOWRANK, GLA_LOWRANK, A_WIDTH, B_WIDTH, KV_B, KV_B, B_WIDTH)

C_QA = 0
C_KA = C_QA + QK_A
C_VA = C_KA + QK_A
C_ZA = C_VA + A_WIDTH
C_QB = C_ZA + A_WIDTH
C_KB = C_QB + B_WIDTH
C_VB = C_KB + KV_B
C_ZB = C_VB + KV_B
C_LR = C_ZB + B_WIDTH
LR_PAD = 128
EV_COLS = C_LR + LR_PAD

LOG2E = 1.4426950408889634
NEG_BIG = -1e30

VMEM_LIMIT = 56 * 1024 * 1024


def _rms(x, g):
    return x * lax.rsqrt(jnp.mean(x * x, axis=-1, keepdims=True) + EPS) * g


def _silu(z):
    return z * jax.nn.sigmoid(z)


def _log_sigmoid(g):
    return jnp.minimum(g, 0.0) - jnp.log(1.0 + jnp.exp(-jnp.abs(g)))


def _const_spec(shape):
    nd = len(shape)
    return pl.BlockSpec(shape, lambda *_: (0,) * nd)


def _ev_in_kernel(x_ref, ng_ref, w_ref, gw_ref, gb_ref, qg_ref, kg_ref, cos_ref, sin_ref,
                  qa_ref, ka_ref, va_ref, la_ref, sza_ref, qb_ref, kb_ref, vt_ref, szb_ref):
    h = _rms(x_ref[...], ng_ref[...]).astype(BF16)

    def proj(c0, width):
        return jnp.dot(h, w_ref[:, c0:c0 + width], preferred_element_type=F32)

    qa_ref[...] = (proj(C_QA, QK_A) * (GLA_DK ** -0.5)).astype(BF16)
    ka_ref[...] = proj(C_KA, QK_A).astype(BF16)
    va_ref[...] = proj(C_VA, A_WIDTH).astype(BF16)
    sza_ref[...] = _silu(proj(C_ZA, A_WIDTH)).astype(BF16)
    szb_ref[...] = _silu(proj(C_ZB, B_WIDTH)).astype(BF16)

    lr = proj(C_LR, LR_PAD).astype(BF16)
    gate = jnp.dot(lr, gw_ref[...], preferred_element_type=F32) + gb_ref[...]
    la_ref[...] = _log_sigmoid(gate) * (1.0 / GLA_TAU)

    cos = cos_ref[...]
    sin = sin_ref[...]

    def norm_rope(xh, g):
        xn = _rms(xh, g)
        return xn * cos + pltpu.roll(xn, ATT_HD // 2, 1) * sin

    qscale = (ATT_HD ** -0.5) * LOG2E
    qb = proj(C_QB, B_WIDTH)
    for hh in range(ATT_HEADS):
        sl = slice(hh * ATT_HD, (hh + 1) * ATT_HD)
        qb_ref[:, sl] = (norm_rope(qb[:, sl], qg_ref[...]) * qscale).astype(BF16)
    kb = proj(C_KB, KV_B)
    for hh in range(ATT_KV_HEADS):
        sl = slice(hh * ATT_HD, (hh + 1) * ATT_HD)
        kb_ref[:, sl] = norm_rope(kb[:, sl], kg_ref[...]).astype(BF16)

    vb = proj(C_VB, KV_B)
    tm = vb.shape[0]
    ones_row = (lax.broadcasted_iota(jnp.int32, (VT_ROWS - ATT_HD, tm), 0) == 0).astype(BF16)
    for hh in range(ATT_KV_HEADS):
        vt_ref[hh * VT_ROWS:hh * VT_ROWS + ATT_HD, :] = vb[:, hh * ATT_HD:(hh + 1) * ATT_HD].T.astype(BF16)
        vt_ref[hh * VT_ROWS + ATT_HD:(hh + 1) * VT_ROWS, :] = ones_row


def _ev_in(x, ng, w, gw, gb, qg, kg, cos2, sin2, *, tm):
    B, S, D = x.shape
    nt = S // tm
    row = lambda width: pl.BlockSpec((None, tm, width), lambda b, i: (b, i, 0))
    outs = [
        (QK_A, BF16), (QK_A, BF16), (A_WIDTH, BF16), (2 * QK_A, F32), (A_WIDTH, BF16),
        (B_WIDTH, BF16), (KV_B, BF16), None, (B_WIDTH, BF16),
    ]
    out_shape, out_specs = [], []
    for o in outs:
        if o is None:
            out_shape.append(jax.ShapeDtypeStruct((B, ATT_KV_HEADS * VT_ROWS, S), BF16))
            out_specs.append(pl.BlockSpec((None, ATT_KV_HEADS * VT_ROWS, tm), lambda b, i: (b, 0, i)))
        else:
            out_shape.append(jax.ShapeDtypeStruct((B, S, o[0]), o[1]))
            out_specs.append(row(o[0]))
    return pl.pallas_call(
        _ev_in_kernel,
        grid=(B, nt),
        in_specs=[
            row(D), _const_spec((1, D)), _const_spec((D, EV_COLS)), _const_spec((LR_PAD, 2 * QK_A)),
            _const_spec((1, 2 * QK_A)), _const_spec((1, ATT_HD)), _const_spec((1, ATT_HD)),
            pl.BlockSpec((tm, ATT_HD), lambda b, i: (i, 0)), pl.BlockSpec((tm, ATT_HD), lambda b, i: (i, 0)),
        ],
        out_specs=out_specs,
        out_shape=out_shape,
        compiler_params=pltpu.CompilerParams(
            dimension_semantics=("parallel", "parallel"), vmem_limit_bytes=VMEM_LIMIT),
        name="ev_in",
    )(x, ng, w, gw, gb, qg, kg, cos2, sin2)


def _split_hi_lo(a):
    hi = a.astype(BF16)
    lo = (a - hi.astype(F32)).astype(BF16)
    return hi, lo


def _gla_kernel(*refs, reverse, final, n_super):
    if final:
        q_ref, k_ref, v_ref, la_ref, ob_ref, sz_ref, g_ref, o_ref, st_ref = refs
    else:
        q_ref, k_ref, v_ref, la_ref, o_ref, st_ref = refs
    SC, C = GLA_SUPER, GLA_CHUNK
    n_chunk = SC // C

    @pl.when(pl.program_id(2) == 0)
    def _():
        st_ref[...] = jnp.zeros_like(st_ref)

    ri = lax.broadcasted_iota(jnp.int32, (SC, SC), 0)
    ci = lax.broadcasted_iota(jnp.int32, (SC, SC), 1)
    same = (ri // C) == (ci // C)
    if reverse:
        incl = same & (ci >= ri)
        incl_t = same & (ri >= ci)
        strict_t = same & (ri < ci)
    else:
        incl = same & (ci <= ri)
        incl_t = same & (ri <= ci)
        strict_t = same & (ri > ci)
    incl_bf = incl.astype(BF16)
    incl_t_bf = incl_t.astype(BF16)
    strict_t_bf = strict_t.astype(BF16)
    lane_chunk = lax.broadcasted_iota(jnp.int32, (GLA_DK, SC), 1) // C

    order = range(n_super - 1, -1, -1) if reverse else range(n_super)
    for sc in order:
        rows = slice(sc * SC, (sc + 1) * SC)
        la = la_ref[rows, :]
        q = q_ref[rows, :].astype(F32)
        k = k_ref[rows, :].astype(F32)
        v = v_ref[rows, :]
        la_t = la.T
        k_t = k.T

        hi, lo = _split_hi_lo(la)
        b2 = jnp.dot(incl_bf, jnp.concatenate([hi, lo], axis=1), preferred_element_type=F32)
        b = b2[:, :GLA_DK] + b2[:, GLA_DK:]
        hi_t, lo_t = _split_hi_lo(la_t)
        hl_t = jnp.concatenate([hi_t, lo_t], axis=0)
        bt2 = jnp.dot(hl_t, incl_t_bf, preferred_element_type=F32)
        b_t = bt2[:GLA_DK] + bt2[GLA_DK:]
        gt2 = jnp.dot(hl_t, strict_t_bf, preferred_element_type=F32)
        g_t = gt2[:GLA_DK] + gt2[GLA_DK:]

        tot_t = b_t + g_t

        q_dec = (q * jnp.exp(b)).astype(BF16)
        k_inv_t = (k_t * jnp.exp(-b_t)).astype(BF16)
        k_end_t = k_t * jnp.exp(g_t)

        scores = jnp.dot(q_dec, k_inv_t, preferred_element_type=F32)
        scores = jnp.where(incl, scores, 0.0).astype(BF16)
        o_sc = jnp.dot(scores, v, preferred_element_type=F32)

        inter = [None] * n_chunk
        corder = range(n_chunk - 1, -1, -1) if reverse else range(n_chunk)
        for c in corder:
            cs = slice(c * C, (c + 1) * C)
            state = st_ref[...]
            inter[c] = jnp.dot(q_dec[cs, :], state.astype(BF16), preferred_element_type=F32)
            k_end_c = jnp.where(lane_chunk == c, k_end_t, 0.0).astype(BF16)
            kv = jnp.dot(k_end_c, v, preferred_element_type=F32)
            st_ref[...] = state * jnp.exp(tot_t[:, c * C:c * C + 1]) + kv
        o_sc = o_sc + jnp.concatenate(inter, axis=0)

        if final:
            o = o_sc + ob_ref[rows, :]
            o = _rms(o, g_ref[...]) * sz_ref[rows, :].astype(F32)
            o_ref[rows, :] = o.astype(o_ref.dtype)
        else:
            o_ref[rows, :] = o_sc.astype(o_ref.dtype)


def _gla(q, k, v, la, *, reverse, rows, extra=None):
    B, S, _ = q.shape
    nb = S // rows
    final = extra is not None
    blk = (lambda i: nb - 1 - i) if reverse else (lambda i: i)
    la_off = GLA_HEADS if reverse else 0
    qk_spec = pl.BlockSpec((None, rows, GLA_DK), lambda b, h, i: (b, blk(i), h))
    v_spec = pl.BlockSpec((None, rows, GLA_DV), lambda b, h, i: (b, blk(i), h))
    la_spec = pl.BlockSpec((None, rows, GLA_DK), lambda b, h, i: (b, blk(i), h + la_off))
    in_specs = [qk_spec, qk_spec, v_spec, la_spec]
    args = [q, k, v, la]
    if final:
        ob, sz, g = extra
        in_specs += [v_spec, v_spec, pl.BlockSpec((1, GLA_DV), lambda b, h, i: (0, h))]
        args += [ob, sz, g]
    return pl.pallas_call(
        functools.partial(_gla_kernel, reverse=reverse, final=final, n_super=rows // GLA_SUPER),
        grid=(B, GLA_HEADS, nb),
        in_specs=in_specs,
        out_specs=v_spec,
        out_shape=jax.ShapeDtypeStruct((B, S, A_WIDTH), BF16 if final else F32),
        scratch_shapes=[pltpu.VMEM((GLA_DK, GLA_DV), F32)],
        compiler_params=pltpu.CompilerParams(
            dimension_semantics=("parallel", "parallel", "arbitrary"), vmem_limit_bytes=VMEM_LIMIT),
        name="gla_fwd" if final else "gla_bwd",
    )(*args)


def _attn_kernel(q_ref, k_ref, vt_ref, sz_ref, o_ref, qt_ref, acc_ref, m_ref, *, tk):
    S = k_ref.shape[0]
    tq = q_ref.shape[0]
    for hh in range(ATT_GROUP):
        qt_ref[hh] = q_ref[:, hh * ATT_HD:(hh + 1) * ATT_HD].astype(F32).T.astype(BF16)
    acc_ref[...] = jnp.zeros_like(acc_ref)
    m_ref[...] = jnp.full_like(m_ref, NEG_BIG)

    def step(j, carry):
        off = pl.multiple_of(j * tk, tk)
        kc = k_ref[pl.ds(off, tk), :]
        vc = vt_ref[:, pl.ds(off, tk)]
        for hh in range(ATT_GROUP):
            s = jnp.dot(kc, qt_ref[hh], preferred_element_type=F32)
            m_old = m_ref[hh]
            m_new = jnp.maximum(m_old, jnp.max(s, axis=0, keepdims=True))
            p = jnp.exp2(s - m_new).astype(BF16)
            alpha = jnp.exp2(m_old - m_new)
            acc_ref[hh] = acc_ref[hh] * alpha + jnp.dot(vc, p, preferred_element_type=F32)
            m_ref[hh] = m_new
        return carry

    lax.fori_loop(0, S // tk, step, 0)

    for hh in range(ATT_GROUP):
        acc = acc_ref[hh]
        o_t = acc[:ATT_HD, :] * (1.0 / acc[ATT_HD:ATT_HD + 1, :])
        sl = slice(hh * ATT_HD, (hh + 1) * ATT_HD)
        o_ref[:, sl] = (o_t.T * sz_ref[:, sl].astype(F32)).astype(o_ref.dtype)


def _attention(qb, kb, vt, szb, *, tq, tk):
    B, S, _ = qb.shape
    gw = ATT_GROUP * ATT_HD
    q_spec = pl.BlockSpec((None, tq, gw), lambda b, kh, i: (b, i, kh))
    return pl.pallas_call(
        functools.partial(_attn_kernel, tk=tk),
        grid=(B, ATT_KV_HEADS, S // tq),
        in_specs=[
            q_spec,
            pl.BlockSpec((None, S, ATT_HD), lambda b, kh, i: (b, 0, kh)),
            pl.BlockSpec((None, VT_ROWS, S), lambda b, kh, i: (b, kh, 0)),
            q_spec,
        ],
        out_specs=q_spec,
        out_shape=jax.ShapeDtypeStruct((B, S, B_WIDTH), BF16),
        scratch_shapes=[
            pltpu.VMEM((ATT_GROUP, ATT_HD, tq), BF16),
            pltpu.VMEM((ATT_GROUP, VT_ROWS, tq), F32),
            pltpu.VMEM((ATT_GROUP, 1, tq), F32),
        ],
        compiler_params=pltpu.CompilerParams(
            dimension_semantics=("parallel", "parallel", "arbitrary"), vmem_limit_bytes=VMEM_LIMIT),
        name="attn",
    )(qb, kb, vt, szb)


def _ev_out_kernel(x_ref, ma_ref, mb_ref, w_ref, o_ref):
    acc = jnp.dot(ma_ref[...], w_ref[:A_WIDTH, :], preferred_element_type=F32)
    acc = acc + jnp.dot(mb_ref[...], w_ref[A_WIDTH:, :], preferred_element_type=F32)
    o_ref[...] = x_ref[...] + acc


def _ev_out(x, ma, mb, w, *, tm):
    B, S, D = x.shape
    row = lambda width: pl.BlockSpec((None, tm, width), lambda b, i: (b, i, 0))
    return pl.pallas_call(
        _ev_out_kernel,
        grid=(B, S // tm),
        in_specs=[row(D), row(A_WIDTH), row(B_WIDTH), _const_spec((A_WIDTH + B_WIDTH, D))],
        out_specs=row(D),
        out_shape=jax.ShapeDtypeStruct((B, S, D), F32),
        compiler_params=pltpu.CompilerParams(
            dimension_semantics=("parallel", "parallel"), vmem_limit_bytes=VMEM_LIMIT),
        name="ev_out",
    )(x, ma, mb, w)


def _odd_kernel(*refs, seq_len, final):
    if final:
        xp_ref, x_ref, xn_ref, ng_ref, win_ref, pw_ref, pb_ref, ps_ref, wout_ref, fg_ref, o_ref = refs
    else:
        xp_ref, x_ref, xn_ref, ng_ref, win_ref, pw_ref, pb_ref, ps_ref, wout_ref, o_ref = refs
    tm = x_ref.shape[0]
    H = POOL_HALO
    t0 = pl.program_id(1) * tm
    x = x_ref[...]
    xe = jnp.concatenate([x, xp_ref[...], xn_ref[...]], axis=0)
    h = _rms(xe, ng_ref[...]).astype(BF16)
    ue = jnp.dot(h, win_ref[:, :POOL_WIDTH], preferred_element_type=F32)
    z = jnp.dot(h[:tm], win_ref[:, POOL_WIDTH:], preferred_element_type=F32)
    ue_bf = ue.astype(BF16)

    t = t0 + lax.broadcasted_iota(jnp.int32, (tm, 1), 0)
    c = lax.broadcasted_iota(jnp.int32, (1, tm + 2 * H), 1)
    pos = t0 + jnp.where(c < tm, c, jnp.where(c < tm + H, c - tm - H, c - H))
    valid = (pos >= 0) & (pos < seq_len)

    ms = []
    for g, w in enumerate(POOL_WINDOWS):
        lo = t - w // 2
        hi = lo + w
        band = ((pos >= lo) & (pos < hi) & valid).astype(BF16)
        cnt = (jnp.minimum(hi, seq_len) - jnp.maximum(lo, 0)).astype(F32)
        cols = slice(g * POOL_GW, (g + 1) * POOL_GW)
        pooled = jnp.dot(band, ue_bf[:, cols], preferred_element_type=F32) / cnt
        d = (pooled - ue[:tm, cols]).astype(BF16)
        ms.append(jnp.dot(d, pw_ref[g], preferred_element_type=F32))
    m = jnp.concatenate(ms, axis=1) + pb_ref[...]
    m = (m * ps_ref[...] * _silu(z)).astype(BF16)
    y = x + jnp.dot(m, wout_ref[...], preferred_element_type=F32)
    if final:
        y = _rms(y, fg_ref[...])
    o_ref[...] = y


def _odd(x, ng, win, pw, pb, ps, wout, fg, *, tm):
    B, S, D = x.shape
    H = POOL_HALO
    nh = tm // H
    last = S // H - 1
    row = pl.BlockSpec((None, tm, D), lambda b, i: (b, i, 0))
    prev = pl.BlockSpec((None, H, D), lambda b, i: (b, jnp.maximum(i * nh - 1, 0), 0))
    nxt = pl.BlockSpec((None, H, D), lambda b, i: (b, jnp.minimum((i + 1) * nh, last), 0))
    final = fg is not None
    in_specs = [
        prev, row, nxt, _const_spec((1, D)), _const_spec((D, 2 * POOL_WIDTH)),
        _const_spec((POOL_GROUPS, POOL_GW, POOL_GW)), _const_spec((1, POOL_WIDTH)),
        _const_spec((1, POOL_WIDTH)), _const_spec((POOL_WIDTH, D)),
    ]
    args = [x, x, x, ng, win, pw, pb, ps, wout]
    if final:
        in_specs.append(_const_spec((1, D)))
        args.append(fg)
    return pl.pallas_call(
        functools.partial(_odd_kernel, seq_len=S, final=final),
        grid=(B, S // tm),
        in_specs=in_specs,
        out_specs=row,
        out_shape=jax.ShapeDtypeStruct((B, S, D), F32),
        compiler_params=pltpu.CompilerParams(
            dimension_semantics=("parallel", "parallel"), vmem_limit_bytes=VMEM_LIMIT),
        name="odd_final" if final else "odd",
    )(*args)


def _rope_tables(S):
    rows = S // GRID_W
    row = jnp.repeat(jnp.arange(rows, dtype=F32), GRID_W)
    col = jnp.tile(jnp.arange(GRID_W, dtype=F32), rows)
    n_pairs = ATT_HD // 4
    freqs = ROPE_THETA ** (-jnp.arange(n_pairs, dtype=F32) / n_pairs)
    ang = jnp.concatenate([row[:, None] * freqs, col[:, None] * freqs], axis=-1)
    c, s = jnp.cos(ang), jnp.sin(ang)
    return jnp.concatenate([c, c], axis=-1), jnp.concatenate([-s, s], axis=-1)


_PAIR_PERM = np.concatenate([np.arange(0, ATT_HD, 2), np.arange(1, ATT_HD, 2)])


def _even_weights(w_in, gate_w, gate_b, q_norm_g, k_norm_g):
    offs = np.concatenate([[0], np.cumsum(EV_SPLITS)])
    seg = lambda i: w_in[:, offs[i]:offs[i + 1]]
    q_a, k_a, v_a, lr_f, lr_b, z_a, q_b, k_b, v_b, z_b = [seg(i) for i in range(10)]
    D = w_in.shape[0]

    def perm_heads(w, nh):
        return w.reshape(D, nh, ATT_HD)[:, :, _PAIR_PERM].reshape(D, nh * ATT_HD)

    lr = jnp.concatenate([lr_f, lr_b, jnp.zeros((D, LR_PAD - 2 * GLA_LOWRANK), w_in.dtype)], axis=1)
    w = jnp.concatenate(
        [q_a, k_a, v_a, z_a, perm_heads(q_b, ATT_HEADS), perm_heads(k_b, ATT_KV_HEADS), v_b, z_b, lr], axis=1)
    gw = jnp.zeros((LR_PAD, 2 * QK_A), F32)
    gw = gw.at[:GLA_LOWRANK, :QK_A].set(gate_w[0])
    gw = gw.at[GLA_LOWRANK:2 * GLA_LOWRANK, QK_A:].set(gate_w[1])
    gb = jnp.concatenate([gate_b[0], gate_b[1]])[None, :]
    return (w.astype(BF16), gw.astype(BF16), gb.astype(F32),
            q_norm_g[_PAIR_PERM][None, :], k_norm_g[_PAIR_PERM][None, :])


def _trunk(x, norm_g, final_norm_g, ev_w_in, ev_gla_gate_w, ev_gla_gate_b, ev_gla_norm_g,
           ev_q_norm_g, ev_k_norm_g, ev_w_out, od_w_in, od_pool_w, od_pool_b, od_pool_scale, od_w_out):
    B, S, D = x.shape
    cos2, sin2 = _rope_tables(S)
    tm_in = min(256, S)
    tm_out = min(512, S)
    tm_odd = min(256, S)
    gla_rows = min(512, S)
    tq = min(256, S)
    tk = min(512, S)
    for layer in range(DEPTH):
        i = layer // 2
        ng = norm_g[layer][None, :]
        if layer % 2 == 0:
            w, gw, gb, qg, kg = _even_weights(ev_w_in[i], ev_gla_gate_w[i], ev_gla_gate_b[i],
                                              ev_q_norm_g[i], ev_k_norm_g[i])
            qa, ka, va, la, sza, qb, kb, vt, szb = _ev_in(x, ng, w, gw, gb, qg, kg, cos2, sin2, tm=tm_in)
            o_bwd = _gla(qa, ka, va, la, reverse=True, rows=gla_rows)
            ma = _gla(qa, ka, va, la, reverse=False, rows=gla_rows,
                      extra=(o_bwd, sza, ev_gla_norm_g[i][None, :]))
            mb = _attention(qb, kb, vt, szb, tq=tq, tk=tk)
            x = _ev_out(x, ma, mb, ev_w_out[i].astype(BF16), tm=tm_out)
        else:
            fg = final_norm_g[None, :] if layer == DEPTH - 1 else None
            x = _odd(x, ng, od_w_in[i].astype(BF16), od_pool_w[i].astype(BF16), od_pool_b[i][None, :],
                     od_pool_scale[i][None, :], od_w_out[i].astype(BF16), fg, tm=tm_odd)
    return x


def kernel(x_prompt, x_sample, norm_g, final_norm_g, ev_w_in, ev_gla_gate_w, ev_gla_gate_b,
           ev_gla_norm_g, ev_q_norm_g, ev_k_norm_g, ev_w_out, od_w_in, od_pool_w, od_pool_b,
           od_pool_scale, od_w_out):
    nb = x_prompt.shape[0]
    x = jnp.concatenate([x_prompt, x_sample], axis=0)
    y = _trunk(x, norm_g, final_norm_g, ev_w_in, ev_gla_gate_w, ev_gla_gate_b, ev_gla_norm_g,
               ev_q_norm_g, ev_k_norm_g, ev_w_out, od_w_in, od_pool_w, od_pool_b, od_pool_scale, od_w_out)
    return (y[:nb], y[nb:])
```

```python
import functools

import numpy as np
import jax
import jax.numpy as jnp
from jax import lax
from jax.experimental import pallas as pl
from jax.experimental.pallas import tpu as pltpu

F32 = jnp.float32
BF16 = jnp.bfloat16

D_MODEL = 1024
DEPTH = 4
GRID_W = 64
EPS = 1e-6

GLA_HEADS = 4
GLA_DK = 128
GLA_DV = 256
GLA_LOWRANK = 16
GLA_TAU = 16.0
GLA_CHUNK = 64
GLA_SUPER = 256

ATT_HEADS = 8
ATT_KV_HEADS = 2
ATT_HD = 128
ATT_GROUP = ATT_HEADS // ATT_KV_HEADS
ROPE_THETA = 10000.0
VT_ROWS = ATT_HD + 16
ATT_UNROLL = 4

POOL_WINDOWS = (2, 4, 8, 16)
POOL_GROUPS = 4
POOL_WIDTH = 2 * D_MODEL
POOL_GW = POOL_WIDTH // POOL_GROUPS
POOL_HALO = 8

A_WIDTH = GLA_HEADS * GLA_DV
B_WIDTH = ATT_HEADS * ATT_HD
QK_A = GLA_HEADS * GLA_DK
KV_B = ATT_KV_HEADS * ATT_HD
EV_SPLITS = (QK_A, QK_A, A_WIDTH, GLA_LOWRANK, GLA_LOWRANK, A_WIDTH, B_WIDTH, KV_B, KV_B, B_WIDTH)

C_QA = 0
C_KA = C_QA + QK_A
C_VA = C_KA + QK_A
C_ZA = C_VA + A_WIDTH
C_QB = C_ZA + A_WIDTH
C_KB = C_QB + B_WIDTH
C_VB = C_KB + KV_B
C_ZB = C_VB + KV_B
C_LR = C_ZB + B_WIDTH
LR_PAD = 128
EV_COLS = C_LR + LR_PAD

LOG2E = 1.4426950408889634
NEG_BIG = -1e30

VMEM_LIMIT = 56 * 1024 * 1024


def _rms(x, g):
    return x * lax.rsqrt(jnp.mean(x * x, axis=-1, keepdims=True) + EPS) * g


def _silu(z):
    return z * jax.nn.sigmoid(z)


def _log_sigmoid(g):
    return jnp.minimum(g, 0.0) - jnp.log(1.0 + jnp.exp(-jnp.abs(g)))


def _const_spec(shape):
    nd = len(shape)
    return pl.BlockSpec(shape, lambda *_: (0,) * nd)


def _ev_in_kernel(x_ref, ng_ref, w_ref, gw_ref, gb_ref, qg_ref, kg_ref, cos_ref, sin_ref,
                  qa_ref, ka_ref, va_ref, la_ref, sza_ref, qb_ref, kb_ref, vt_ref, szb_ref):
    h = _rms(x_ref[...], ng_ref[...]).astype(BF16)

    def proj(c0, width):
        return jnp.dot(h, w_ref[:, c0:c0 + width], preferred_element_type=F32)

    qa_ref[...] = (proj(C_QA, QK_A) * (GLA_DK ** -0.5)).astype(BF16)
    ka_ref[...] = proj(C_KA, QK_A).astype(BF16)
    va_ref[...] = proj(C_VA, A_WIDTH).astype(BF16)
    sza_ref[...] = _silu(proj(C_ZA, A_WIDTH)).astype(BF16)
    szb_ref[...] = _silu(proj(C_ZB, B_WIDTH)).astype(BF16)

    lr = proj(C_LR, LR_PAD).astype(BF16)
    gate = jnp.dot(lr, gw_ref[...], preferred_element_type=F32) + gb_ref[...]
    la_ref[...] = _log_sigmoid(gate) * (1.0 / GLA_TAU)

    cos = cos_ref[...]
    sin = sin_ref[...]

    def norm_rope(xh, g):
        xn = _rms(xh, g)
        return xn * cos + pltpu.roll(xn, ATT_HD // 2, 1) * sin

    qscale = (ATT_HD ** -0.5) * LOG2E
    qb = proj(C_QB, B_WIDTH)
    for hh in range(ATT_HEADS):
        sl = slice(hh * ATT_HD, (hh + 1) * ATT_HD)
        qb_ref[:, sl] = (norm_rope(qb[:, sl], qg_ref[...]) * qscale).astype(BF16)
    kb = proj(C_KB, KV_B)
    for hh in range(ATT_KV_HEADS):
        sl = slice(hh * ATT_HD, (hh + 1) * ATT_HD)
        kb_ref[:, sl] = norm_rope(kb[:, sl], kg_ref[...]).astype(BF16)

    vb = proj(C_VB, KV_B)
    tm = vb.shape[0]
    ones_row = (lax.broadcasted_iota(jnp.int32, (VT_ROWS - ATT_HD, tm), 0) == 0).astype(BF16)
    for hh in range(ATT_KV_HEADS):
        vt_ref[hh * VT_ROWS:hh * VT_ROWS + ATT_HD, :] = vb[:, hh * ATT_HD:(hh + 1) * ATT_HD].T.astype(BF16)
        vt_ref[hh * VT_ROWS + ATT_HD:(hh + 1) * VT_ROWS, :] = ones_row


def _ev_in(x, ng, w, gw, gb, qg, kg, cos2, sin2, *, tm):
    B, S, D = x.shape
    nt = S // tm
    row = lambda width: pl.BlockSpec((None, tm, width), lambda b, i: (b, i, 0))
    outs = [
        (QK_A, BF16), (QK_A, BF16), (A_WIDTH, BF16), (2 * QK_A, F32), (A_WIDTH, BF16),
        (B_WIDTH, BF16), (KV_B, BF16), None, (B_WIDTH, BF16),
    ]
    out_shape, out_specs = [], []
    for o in outs:
        if o is None:
            out_shape.append(jax.ShapeDtypeStruct((B, ATT_KV_HEADS * VT_ROWS, S), BF16))
            out_specs.append(pl.BlockSpec((None, ATT_KV_HEADS * VT_ROWS, tm), lambda b, i: (b, 0, i)))
        else:
            out_shape.append(jax.ShapeDtypeStruct((B, S, o[0]), o[1]))
            out_specs.append(row(o[0]))
    return pl.pallas_call(
        _ev_in_kernel,
        grid=(B, nt),
        in_specs=[
            row(D), _const_spec((1, D)), _const_spec((D, EV_COLS)), _const_spec((LR_PAD, 2 * QK_A)),
            _const_spec((1, 2 * QK_A)), _const_spec((1, ATT_HD)), _const_spec((1, ATT_HD)),
            pl.BlockSpec((tm, ATT_HD), lambda b, i: (i, 0)), pl.BlockSpec((tm, ATT_HD), lambda b, i: (i, 0)),
        ],
        out_specs=out_specs,
        out_shape=out_shape,
        compiler_params=pltpu.CompilerParams(
            dimension_semantics=("parallel", "parallel"), vmem_limit_bytes=VMEM_LIMIT),
        name="ev_in",
    )(x, ng, w, gw, gb, qg, kg, cos2, sin2)


def _split_hi_lo(a):
    hi = a.astype(BF16)
    lo = (a - hi.astype(F32)).astype(BF16)
    return hi, lo


def _gla_kernel(*refs, reverse, final, n_super):
    if final:
        q_ref, k_ref, v_ref, la_ref, ob_ref, sz_ref, g_ref, o_ref, st_ref = refs
    else:
        q_ref, k_ref, v_ref, la_ref, o_ref, st_ref = refs
    SC, C = GLA_SUPER, GLA_CHUNK
    n_chunk = SC // C

    @pl.when(pl.program_id(1) == 0)
    def _():
        st_ref[...] = jnp.zeros_like(st_ref)

    ri = lax.broadcasted_iota(jnp.int32, (SC, SC), 0)
    ci = lax.broadcasted_iota(jnp.int32, (SC, SC), 1)
    same = (ri // C) == (ci // C)
    if reverse:
        incl = same & (ci >= ri)
        incl_t = same & (ri >= ci)
        strict_t = same & (ri < ci)
    else:
        incl = same & (ci <= ri)
        incl_t = same & (ri <= ci)
        strict_t = same & (ri > ci)
    incl_bf = incl.astype(BF16)
    incl_t_bf = incl_t.astype(BF16)
    strict_t_bf = strict_t.astype(BF16)
    lane_chunk = lax.broadcasted_iota(jnp.int32, (GLA_DK, SC), 1) // C

    order = range(n_super - 1, -1, -1) if reverse else range(n_super)
    for sc in order:
        rows = slice(sc * SC, (sc + 1) * SC)
        for hd in range(GLA_HEADS):
            kc = slice(hd * GLA_DK, (hd + 1) * GLA_DK)
            vc = slice(hd * GLA_DV, (hd + 1) * GLA_DV)
            la = la_ref[rows, kc]
            q = q_ref[rows, kc].astype(F32)
            k = k_ref[rows, kc].astype(F32)
            v = v_ref[rows, vc]
            la_t = la.T
            k_t = k.T

            hi, lo = _split_hi_lo(la)
            b2 = jnp.dot(incl_bf, jnp.concatenate([hi, lo], axis=1), preferred_element_type=F32)
            b = b2[:, :GLA_DK] + b2[:, GLA_DK:]
            hi_t, lo_t = _split_hi_lo(la_t)
            hl_t = jnp.concatenate([hi_t, lo_t], axis=0)
            bt2 = jnp.dot(hl_t, incl_t_bf, preferred_element_type=F32)
            b_t = bt2[:GLA_DK] + bt2[GLA_DK:]
            gt2 = jnp.dot(hl_t, strict_t_bf, preferred_element_type=F32)
            g_t = gt2[:GLA_DK] + gt2[GLA_DK:]
            tot_t = b_t + g_t

            q_dec = (q * jnp.exp(b)).astype(BF16)
            k_inv_t = (k_t * jnp.exp(-b_t)).astype(BF16)
            k_end_t = k_t * jnp.exp(g_t)

            scores = jnp.dot(q_dec, k_inv_t, preferred_element_type=F32)
            scores = jnp.where(incl, scores, 0.0).astype(BF16)
            o_sc = jnp.dot(scores, v, preferred_element_type=F32)

            inter = [None] * n_chunk
            corder = range(n_chunk - 1, -1, -1) if reverse else range(n_chunk)
            for c in corder:
                cs = slice(c * C, (c + 1) * C)
                state = st_ref[hd]
                inter[c] = jnp.dot(q_dec[cs, :], state.astype(BF16), preferred_element_type=F32)
                k_end_c = jnp.where(lane_chunk == c, k_end_t, 0.0).astype(BF16)
                kv = jnp.dot(k_end_c, v, preferred_element_type=F32)
                st_ref[hd] = state * jnp.exp(tot_t[:, c * C:c * C + 1]) + kv
            o_sc = o_sc + jnp.concatenate(inter, axis=0)

            if final:
                o = o_sc + ob_ref[rows, vc]
                o = _rms(o, g_ref[:, vc]) * sz_ref[rows, vc].astype(F32)
                o_ref[rows, vc] = o.astype(o_ref.dtype)
            else:
                o_ref[rows, vc] = o_sc.astype(o_ref.dtype)


def _gla(q, k, v, la, *, reverse, rows, extra=None):
    B, S, _ = q.shape
    nb = S // rows
    final = extra is not None
    blk = (lambda i: nb - 1 - i) if reverse else (lambda i: i)
    la_blk = 1 if reverse else 0
    qk_spec = pl.BlockSpec((None, rows, QK_A), lambda b, i: (b, blk(i), 0))
    v_spec = pl.BlockSpec((None, rows, A_WIDTH), lambda b, i: (b, blk(i), 0))
    la_spec = pl.BlockSpec((None, rows, QK_A), lambda b, i: (b, blk(i), la_blk))
    in_specs = [qk_spec, qk_spec, v_spec, la_spec]
    args = [q, k, v, la]
    if final:
        ob, sz, g = extra
        in_specs += [v_spec, v_spec, _const_spec((1, A_WIDTH))]
        args += [ob, sz, g]
    return pl.pallas_call(
        functools.partial(_gla_kernel, reverse=reverse, final=final, n_super=rows // GLA_SUPER),
        grid=(B, nb),
        in_specs=in_specs,
        out_specs=v_spec,
        out_shape=jax.ShapeDtypeStruct((B, S, A_WIDTH), BF16 if final else F32),
        scratch_shapes=[pltpu.VMEM((GLA_HEADS, GLA_DK, GLA_DV), F32)],
        compiler_params=pltpu.CompilerParams(
            dimension_semantics=("parallel", "arbitrary"), vmem_limit_bytes=VMEM_LIMIT),
        name="gla_fwd" if final else "gla_bwd",
    )(*args)


def _attn_kernel(q_ref, k_ref, vt_ref, sz_ref, o_ref, qt_ref, s_ref, mx_ref, acc_ref, m_ref, *, tk):
    S = k_ref.shape[0]
    tq = q_ref.shape[0]
    nk = S // tk
    for hh in range(ATT_GROUP):
        qt_ref[:, hh * tq:(hh + 1) * tq] = q_ref[:, hh * ATT_HD:(hh + 1) * ATT_HD].astype(F32).T.astype(BF16)
    acc_ref[...] = jnp.zeros_like(acc_ref)
    m_ref[...] = jnp.full_like(m_ref, NEG_BIG)

    def scores(j, slot):
        off = pl.multiple_of(j * tk, tk)
        s = jnp.dot(k_ref[pl.ds(off, tk), :], qt_ref[...], preferred_element_type=F32)
        s_ref[slot] = s
        mx_ref[slot] = jnp.max(s, axis=0, keepdims=True)

    def accumulate(j, slot):
        off = pl.multiple_of(j * tk, tk)
        m_old = m_ref[...]
        m_new = jnp.maximum(m_old, mx_ref[slot])
        p = jnp.exp2(s_ref[slot] - m_new).astype(BF16)
        pv = jnp.dot(vt_ref[:, pl.ds(off, tk)], p, preferred_element_type=F32)
        acc_ref[...] = acc_ref[...] * jnp.exp2(m_old - m_new) + pv
        m_ref[...] = m_new

    scores(0, 0)

    def group(jj, carry):
        j = ATT_UNROLL * jj
        for u in range(ATT_UNROLL):
            scores(jnp.minimum(j + u + 1, nk - 1), (u + 1) % 2)
            accumulate(j + u, u % 2)
        return carry

    lax.fori_loop(0, nk // ATT_UNROLL, group, 0)

    acc = acc_ref[...]
    o_t = acc[:ATT_HD, :] * (1.0 / acc[ATT_HD:ATT_HD + 1, :])
    for hh in range(ATT_GROUP):
        sl = slice(hh * ATT_HD, (hh + 1) * ATT_HD)
        o_ref[:, sl] = (o_t[:, hh * tq:(hh + 1) * tq].T * sz_ref[:, sl].astype(F32)).astype(o_ref.dtype)


def _attention(qb, kb, vt, szb, *, tq, tk):
    B, S, _ = qb.shape
    assert S % tq == 0 and (S // tk) % ATT_UNROLL == 0
    gw = ATT_GROUP * ATT_HD
    q_spec = pl.BlockSpec((None, tq, gw), lambda b, kh, i: (b, i, kh))
    return pl.pallas_call(
        functools.partial(_attn_kernel, tk=tk),
        grid=(B, ATT_KV_HEADS, S // tq),
        in_specs=[
            q_spec,
            pl.BlockSpec((None, S, ATT_HD), lambda b, kh, i: (b, 0, kh)),
            pl.BlockSpec((None, VT_ROWS, S), lambda b, kh, i: (b, kh, 0)),
            q_spec,
        ],
        out_specs=q_spec,
        out_shape=jax.ShapeDtypeStruct((B, S, B_WIDTH), BF16),
        scratch_shapes=[
            pltpu.VMEM((ATT_HD, ATT_GROUP * tq), BF16),
            pltpu.VMEM((2, tk, ATT_GROUP * tq), F32),
            pltpu.VMEM((2, 1, ATT_GROUP * tq), F32),
            pltpu.VMEM((VT_ROWS, ATT_GROUP * tq), F32),
            pltpu.VMEM((1, ATT_GROUP * tq), F32),
        ],
        compiler_params=pltpu.CompilerParams(
            dimension_semantics=("parallel", "parallel", "arbitrary"), vmem_limit_bytes=VMEM_LIMIT),
        name="attn",
    )(qb, kb, vt, szb)


def _ev_out_kernel(x_ref, ma_ref, mb_ref, w_ref, o_ref):
    acc = jnp.dot(ma_ref[...], w_ref[:A_WIDTH, :], preferred_element_type=F32)
    acc = acc + jnp.dot(mb_ref[...], w_ref[A_WIDTH:, :], preferred_element_type=F32)
    o_ref[...] = x_ref[...] + acc


def _ev_out(x, ma, mb, w, *, tm):
    B, S, D = x.shape
    row = lambda width: pl.BlockSpec((None, tm, width), lambda b, i: (b, i, 0))
    return pl.pallas_call(
        _ev_out_kernel,
        grid=(B, S // tm),
        in_specs=[row(D), row(A_WIDTH), row(B_WIDTH), _const_spec((A_WIDTH + B_WIDTH, D))],
        out_specs=row(D),
        out_shape=jax.ShapeDtypeStruct((B, S, D), F32),
        compiler_params=pltpu.CompilerParams(
            dimension_semantics=("parallel", "parallel"), vmem_limit_bytes=VMEM_LIMIT),
        name="ev_out",
    )(x, ma, mb, w)


def _odd_kernel(*refs, seq_len, final):
    if final:
        xp_ref, x_ref, xn_ref, ng_ref, win_ref, pw_ref, pb_ref, ps_ref, wout_ref, fg_ref, o_ref = refs
    else:
        xp_ref, x_ref, xn_ref, ng_ref, win_ref, pw_ref, pb_ref, ps_ref, wout_ref, o_ref = refs
    tm = x_ref.shape[0]
    H = POOL_HALO
    t0 = pl.program_id(1) * tm
    x = x_ref[...]
    xe = jnp.concatenate([x, xp_ref[...], xn_ref[...]], axis=0)
    h = _rms(xe, ng_ref[...]).astype(BF16)
    ue = jnp.dot(h, win_ref[:, :POOL_WIDTH], preferred_element_type=F32)
    z = jnp.dot(h[:tm], win_ref[:, POOL_WIDTH:], preferred_element_type=F32)
    ue_bf = ue.astype(BF16)

    t = t0 + lax.broadcasted_iota(jnp.int32, (tm, 1), 0)
    c = lax.broadcasted_iota(jnp.int32, (1, tm + 2 * H), 1)
    pos = t0 + jnp.where(c < tm, c, jnp.where(c < tm + H, c - tm - H, c - H))
    valid = (pos >= 0) & (pos < seq_len)

    ms = []
    for g, w in enumerate(POOL_WINDOWS):
        lo = t - w // 2
        hi = lo + w
        band = ((pos >= lo) & (pos < hi) & valid).astype(BF16)
        cnt = (jnp.minimum(hi, seq_len) - jnp.maximum(lo, 0)).astype(F32)
        cols = slice(g * POOL_GW, (g + 1) * POOL_GW)
        pooled = jnp.dot(band, ue_bf[:, cols], preferred_element_type=F32) / cnt
        d = (pooled - ue[:tm, cols]).astype(BF16)
        ms.append(jnp.dot(d, pw_ref[g], preferred_element_type=F32))
    m = jnp.concatenate(ms, axis=1) + pb_ref[...]
    m = (m * ps_ref[...] * _silu(z)).astype(BF16)
    y = x + jnp.dot(m, wout_ref[...], preferred_element_type=F32)
    if final:
        y = _rms(y, fg_ref[...])
    o_ref[...] = y


def _odd(x, ng, win, pw, pb, ps, wout, fg, *, tm):
    B, S, D = x.shape
    H = POOL_HALO
    nh = tm // H
    last = S // H - 1
    row = pl.BlockSpec((None, tm, D), lambda b, i: (b, i, 0))
    prev = pl.BlockSpec((None, H, D), lambda b, i: (b, jnp.maximum(i * nh - 1, 0), 0))
    nxt = pl.BlockSpec((None, H, D), lambda b, i: (b, jnp.minimum((i + 1) * nh, last), 0))
    final = fg is not None
    in_specs = [
        prev, row, nxt, _const_spec((1, D)), _const_spec((D, 2 * POOL_WIDTH)),
        _const_spec((POOL_GROUPS, POOL_GW, POOL_GW)), _const_spec((1, POOL_WIDTH)),
        _const_spec((1, POOL_WIDTH)), _const_spec((POOL_WIDTH, D)),
    ]
    args = [x, x, x, ng, win, pw, pb, ps, wout]
    if final:
        in_specs.append(_const_spec((1, D)))
        args.append(fg)
    return pl.pallas_call(
        functools.partial(_odd_kernel, seq_len=S, final=final),
        grid=(B, S // tm),
        in_specs=in_specs,
        out_specs=row,
        out_shape=jax.ShapeDtypeStruct((B, S, D), F32),
        compiler_params=pltpu.CompilerParams(
            dimension_semantics=("parallel", "parallel"), vmem_limit_bytes=VMEM_LIMIT),
        name="odd_final" if final else "odd",
    )(*args)


def _rope_tables(S):
    rows = S // GRID_W
    row = jnp.repeat(jnp.arange(rows, dtype=F32), GRID_W)
    col = jnp.tile(jnp.arange(GRID_W, dtype=F32), rows)
    n_pairs = ATT_HD // 4
    freqs = ROPE_THETA ** (-jnp.arange(n_pairs, dtype=F32) / n_pairs)
    ang = jnp.concatenate([row[:, None] * freqs, col[:, None] * freqs], axis=-1)
    c, s = jnp.cos(ang), jnp.sin(ang)
    return jnp.concatenate([c, c], axis=-1), jnp.concatenate([-s, s], axis=-1)


_PAIR_PERM = np.concatenate([np.arange(0, ATT_HD, 2), np.arange(1, ATT_HD, 2)])


def _even_weights(w_in, gate_w, gate_b, q_norm_g, k_norm_g):
    offs = np.concatenate([[0], np.cumsum(EV_SPLITS)])
    seg = lambda i: w_in[:, offs[i]:offs[i + 1]]
    q_a, k_a, v_a, lr_f, lr_b, z_a, q_b, k_b, v_b, z_b = [seg(i) for i in range(10)]
    D = w_in.shape[0]

    def perm_heads(w, nh):
        return w.reshape(D, nh, ATT_HD)[:, :, _PAIR_PERM].reshape(D, nh * ATT_HD)

    lr = jnp.concatenate([lr_f, lr_b, jnp.zeros((D, LR_PAD - 2 * GLA_LOWRANK), w_in.dtype)], axis=1)
    w = jnp.concatenate(
        [q_a, k_a, v_a, z_a, perm_heads(q_b, ATT_HEADS), perm_heads(k_b, ATT_KV_HEADS), v_b, z_b, lr], axis=1)
    gw = jnp.zeros((LR_PAD, 2 * QK_A), F32)
    gw = gw.at[:GLA_LOWRANK, :QK_A].set(gate_w[0])
    gw = gw.at[GLA_LOWRANK:2 * GLA_LOWRANK, QK_A:].set(gate_w[1])
    gb = jnp.concatenate([gate_b[0], gate_b[1]])[None, :]
    return (w.astype(BF16), gw.astype(BF16), gb.astype(F32),
            q_norm_g[_PAIR_PERM][None, :], k_norm_g[_PAIR_PERM][None, :])


def _trunk(x, norm_g, final_norm_g, ev_w_in, ev_gla_gate_w, ev_gla_gate_b, ev_gla_norm_g,
           ev_q_norm_g, ev_k_norm_g, ev_w_out, od_w_in, od_pool_w, od_pool_b, od_pool_scale, od_w_out):
    B, S, D = x.shape
    cos2, sin2 = _rope_tables(S)
    tm_in = min(256, S)
    tm_out = min(512, S)
    tm_odd = min(256, S)
    gla_rows = min(512, S)
    tq = min(256, S)
    tk = min(512, S)
    for layer in range(DEPTH):
        i = layer // 2
        ng = norm_g[layer][None, :]
        if layer % 2 == 0:
            w, gw, gb, qg, kg = _even_weights(ev_w_in[i], ev_gla_gate_w[i], ev_gla_gate_b[i],
                                              ev_q_norm_g[i], ev_k_norm_g[i])
            qa, ka, va, la, sza, qb, kb, vt, szb = _ev_in(x, ng, w, gw, gb, qg, kg, cos2, sin2, tm=tm_in)
            o_bwd = _gla(qa, ka, va, la, reverse=True, rows=gla_rows)
            ma = _gla(qa, ka, va, la, reverse=False, rows=gla_rows,
                      extra=(o_bwd, sza, ev_gla_norm_g[i][None, :]))
            mb = _attention(qb, kb, vt, szb, tq=tq, tk=tk)
            x = _ev_out(x, ma, mb, ev_w_out[i].astype(BF16), tm=tm_out)
        else:
            fg = final_norm_g[None, :] if layer == DEPTH - 1 else None
            x = _odd(x, ng, od_w_in[i].astype(BF16), od_pool_w[i].astype(BF16), od_pool_b[i][None, :],
                     od_pool_scale[i][None, :], od_w_out[i].astype(BF16), fg, tm=tm_odd)
    return x


def kernel(x_prompt, x_sample, norm_g, final_norm_g, ev_w_in, ev_gla_gate_w, ev_gla_gate_b,
           ev_gla_norm_g, ev_q_norm_g, ev_k_norm_g, ev_w_out, od_w_in, od_pool_w, od_pool_b,
           od_pool_scale, od_w_out):
    nb = x_prompt.shape[0]
    x = jnp.concatenate([x_prompt, x_sample], axis=0)
    y = _trunk(x, norm_g, final_norm_g, ev_w_in, ev_gla_gate_w, ev_gla_gate_b, ev_gla_norm_g,
               ev_q_norm_g, ev_k_norm_g, ev_w_out, od_w_in, od_pool_w, od_pool_b, od_pool_scale, od_w_out)
    return (y[:nb], y[nb:])
```

```python
import functools

import numpy as np
import jax
import jax.numpy as jnp
from jax import lax
from jax.experimental import pallas as pl
from jax.experimental.pallas import tpu as pltpu

F32 = jnp.float32
BF16 = jnp.bfloat16

D_MODEL = 1024
DEPTH = 4
GRID_W = 64
EPS = 1e-6

GLA_HEADS = 4
GLA_DK = 128
GLA_DV = 256
GLA_LOWRANK = 16
GLA_TAU = 16.0
GLA_CHUNK = 64
GLA_SUPER = 256

ATT_HEADS = 8
ATT_KV_HEADS = 2
ATT_HD = 128
ATT_GROUP = ATT_HEADS // ATT_KV_HEADS
ROPE_THETA = 10000.0
VT_ROWS = ATT_HD + 16
ATT_UNROLL = 8

POOL_WINDOWS = (2, 4, 8, 16)
POOL_GROUPS = 4
POOL_WIDTH = 2 * D_MODEL
POOL_GW = POOL_WIDTH // POOL_GROUPS
POOL_HALO = 8

A_WIDTH = GLA_HEADS * GLA_DV
B_WIDTH = ATT_HEADS * ATT_HD
QK_A = GLA_HEADS * GLA_DK
KV_B = ATT_KV_HEADS * ATT_HD
EV_SPLITS = (QK_A, QK_A, A_WIDTH, GLA_LOWRANK, GLA_LOWRANK, A_WIDTH, B_WIDTH, KV_B, KV_B, B_WIDTH)

C_QA = 0
C_KA = C_QA + QK_A
C_VA = C_KA + QK_A
C_ZA = C_VA + A_WIDTH
C_QB = C_ZA + A_WIDTH
C_KB = C_QB + B_WIDTH
C_VB = C_KB + KV_B
C_ZB = C_VB + KV_B
C_LR = C_ZB + B_WIDTH
LR_PAD = 128
EV_COLS = C_LR + LR_PAD

LOG2E = 1.4426950408889634
NEG_BIG = -1e30

VMEM_LIMIT = 56 * 1024 * 1024


def _rms(x, g):
    return x * lax.rsqrt(jnp.mean(x * x, axis=-1, keepdims=True) + EPS) * g


def _silu(z):
    return z * jax.nn.sigmoid(z)


def _log_sigmoid(g):
    return jnp.minimum(g, 0.0) - jnp.log(1.0 + jnp.exp(-jnp.abs(g)))


def _const_spec(shape):
    nd = len(shape)
    return pl.BlockSpec(shape, lambda *_: (0,) * nd)


def _ev_in_kernel(x_ref, ng_ref, w_ref, gw_ref, gb_ref, qg_ref, kg_ref, cos_ref, sin_ref,
                  qa_ref, ka_ref, va_ref, la_ref, sza_ref, qb_ref, kb_ref, vt_ref, szb_ref):
    h = _rms(x_ref[...], ng_ref[...]).astype(BF16)

    def proj(c0, width):
        return jnp.dot(h, w_ref[:, c0:c0 + width], preferred_element_type=F32)

    lr = proj(C_LR, LR_PAD).astype(BF16)
    gate = jnp.dot(lr, gw_ref[...], preferred_element_type=F32) + gb_ref[...]
    la_ref[...] = _log_sigmoid(gate) * (1.0 / GLA_TAU)

    cos = cos_ref[...]
    sin = sin_ref[...]

    def norm_rope(xh, g):
        xn = _rms(xh, g)
        return xn * cos + pltpu.roll(xn, ATT_HD // 2, 1) * sin

    qscale = (ATT_HD ** -0.5) * LOG2E
    qb = proj(C_QB, B_WIDTH)
    for hh in range(ATT_HEADS):
        sl = slice(hh * ATT_HD, (hh + 1) * ATT_HD)
        qb_ref[:, sl] = (norm_rope(qb[:, sl], qg_ref[...]) * qscale).astype(BF16)
    kb = proj(C_KB, KV_B)
    for hh in range(ATT_KV_HEADS):
        sl = slice(hh * ATT_HD, (hh + 1) * ATT_HD)
        kb_ref[:, sl] = norm_rope(kb[:, sl], kg_ref[...]).astype(BF16)

    vb = proj(C_VB, KV_B)
    tm = vb.shape[0]
    ones_row = (lax.broadcasted_iota(jnp.int32, (VT_ROWS - ATT_HD, tm), 0) == 0).astype(BF16)
    for hh in range(ATT_KV_HEADS):
        vt_ref[hh * VT_ROWS:hh * VT_ROWS + ATT_HD, :] = vb[:, hh * ATT_HD:(hh + 1) * ATT_HD].T.astype(BF16)
        vt_ref[hh * VT_ROWS + ATT_HD:(hh + 1) * VT_ROWS, :] = ones_row

    sza_ref[...] = _silu(proj(C_ZA, A_WIDTH)).astype(BF16)
    szb_ref[...] = _silu(proj(C_ZB, B_WIDTH)).astype(BF16)
    qa_ref[...] = (proj(C_QA, QK_A) * (GLA_DK ** -0.5)).astype(BF16)
    ka_ref[...] = proj(C_KA, QK_A).astype(BF16)
    va_ref[...] = proj(C_VA, A_WIDTH).astype(BF16)


def _ev_in(x, ng, w, gw, gb, qg, kg, cos2, sin2, *, tm):
    B, S, D = x.shape
    nt = S // tm
    row = lambda width: pl.BlockSpec((None, tm, width), lambda b, i: (b, i, 0))
    outs = [
        (QK_A, BF16), (QK_A, BF16), (A_WIDTH, BF16), (2 * QK_A, F32), (A_WIDTH, BF16),
        (B_WIDTH, BF16), (KV_B, BF16), None, (B_WIDTH, BF16),
    ]
    out_shape, out_specs = [], []
    for o in outs:
        if o is None:
            out_shape.append(jax.ShapeDtypeStruct((B, ATT_KV_HEADS * VT_ROWS, S), BF16))
            out_specs.append(pl.BlockSpec((None, ATT_KV_HEADS * VT_ROWS, tm), lambda b, i: (b, 0, i)))
        else:
            out_shape.append(jax.ShapeDtypeStruct((B, S, o[0]), o[1]))
            out_specs.append(row(o[0]))
    return pl.pallas_call(
        _ev_in_kernel,
        grid=(B, nt),
        in_specs=[
            row(D), _const_spec((1, D)), _const_spec((D, EV_COLS)), _const_spec((LR_PAD, 2 * QK_A)),
            _const_spec((1, 2 * QK_A)), _const_spec((1, ATT_HD)), _const_spec((1, ATT_HD)),
            pl.BlockSpec((tm, ATT_HD), lambda b, i: (i, 0)), pl.BlockSpec((tm, ATT_HD), lambda b, i: (i, 0)),
        ],
        out_specs=out_specs,
        out_shape=out_shape,
        compiler_params=pltpu.CompilerParams(
            dimension_semantics=("parallel", "parallel"), vmem_limit_bytes=VMEM_LIMIT),
        name="ev_in",
    )(x, ng, w, gw, gb, qg, kg, cos2, sin2)


def _split_hi_lo(a):
    hi = a.astype(BF16)
    lo = (a - hi.astype(F32)).astype(BF16)
    return hi, lo


def _gla_kernel(*refs, reverse, final, n_super):
    if final:
        q_ref, k_ref, v_ref, la_ref, ob_ref, sz_ref, g_ref, o_ref, st_ref = refs
    else:
        q_ref, k_ref, v_ref, la_ref, o_ref, st_ref = refs
    SC, C = GLA_SUPER, GLA_CHUNK
    n_chunk = SC // C

    @pl.when(pl.program_id(1) == 0)
    def _():
        st_ref[...] = jnp.zeros_like(st_ref)

    ri = lax.broadcasted_iota(jnp.int32, (SC, SC), 0)
    ci = lax.broadcasted_iota(jnp.int32, (SC, SC), 1)
    same = (ri // C) == (ci // C)
    if reverse:
        incl = same & (ci >= ri)
        incl_t = same & (ri >= ci)
        strict_t = same & (ri < ci)
    else:
        incl = same & (ci <= ri)
        incl_t = same & (ri <= ci)
        strict_t = same & (ri > ci)
    incl_bf = incl.astype(BF16)
    incl_t_bf = incl_t.astype(BF16)
    strict_t_bf = strict_t.astype(BF16)
    lane_chunk = lax.broadcasted_iota(jnp.int32, (GLA_DK, SC), 1) // C

    order = range(n_super - 1, -1, -1) if reverse else range(n_super)
    corder = range(n_chunk - 1, -1, -1) if reverse else range(n_chunk)
    bodies = [(sc, hd) for sc in order for hd in range(GLA_HEADS)]
    rows = {sc: slice(sc * SC, (sc + 1) * SC) for sc in order}
    kcol = [slice(hd * GLA_DK, (hd + 1) * GLA_DK) for hd in range(GLA_HEADS)]
    vcol = [slice(hd * GLA_DV, (hd + 1) * GLA_DV) for hd in range(GLA_HEADS)]
    dot = functools.partial(jnp.dot, preferred_element_type=F32)

    hl, hl_t, k_t = {}, {}, {}
    for key in bodies:
        sc, hd = key
        la = la_ref[rows[sc], kcol[hd]]
        hi, lo = _split_hi_lo(la)
        hl[key] = jnp.concatenate([hi, lo], axis=1)
        hi_t, lo_t = _split_hi_lo(la.T)
        hl_t[key] = jnp.concatenate([hi_t, lo_t], axis=0)
        k_t[key] = k_ref[rows[sc], kcol[hd]].astype(F32).T
    b2 = {key: dot(incl_bf, hl[key]) for key in bodies}
    bt2 = {key: dot(hl_t[key], incl_t_bf) for key in bodies}
    gt2 = {key: dot(hl_t[key], strict_t_bf) for key in bodies}

    q_dec, k_inv_t, k_end_t, tot_t = {}, {}, {}, {}
    for key in bodies:
        sc, hd = key
        b = b2[key][:, :GLA_DK] + b2[key][:, GLA_DK:]
        b_t = bt2[key][:GLA_DK] + bt2[key][GLA_DK:]
        g_t = gt2[key][:GLA_DK] + gt2[key][GLA_DK:]
        tot_t[key] = b_t + g_t
        q_dec[key] = (q_ref[rows[sc], kcol[hd]].astype(F32) * jnp.exp(b)).astype(BF16)
        k_inv_t[key] = (k_t[key] * jnp.exp(-b_t)).astype(BF16)
        k_end_t[key] = k_t[key] * jnp.exp(g_t)

    scores = {key: dot(q_dec[key], k_inv_t[key]) for key in bodies}
    o_sc = {}
    for key in bodies:
        sc, hd = key
        o_sc[key] = dot(jnp.where(incl, scores[key], 0.0).astype(BF16), v_ref[rows[sc], vcol[hd]])
    kv = {}
    for key in bodies:
        sc, hd = key
        for c in corder:
            k_end_c = jnp.where(lane_chunk == c, k_end_t[key], 0.0).astype(BF16)
            kv[key, c] = dot(k_end_c, v_ref[rows[sc], vcol[hd]])

    inter = {}
    for sc in order:
        for c in corder:
            for hd in range(GLA_HEADS):
                key = (sc, hd)
                state = st_ref[hd]
                inter[key, c] = dot(q_dec[key][c * C:(c + 1) * C, :], state.astype(BF16))
                st_ref[hd] = state * jnp.exp(tot_t[key][:, c * C:c * C + 1]) + kv[key, c]

    for key in bodies:
        sc, hd = key
        o = o_sc[key] + jnp.concatenate([inter[key, c] for c in range(n_chunk)], axis=0)
        if final:
            o = o + ob_ref[rows[sc], vcol[hd]]
            o = _rms(o, g_ref[:, vcol[hd]]) * sz_ref[rows[sc], vcol[hd]].astype(F32)
        o_ref[rows[sc], vcol[hd]] = o.astype(o_ref.dtype)


def _gla(q, k, v, la, *, reverse, rows, extra=None):
    B, S, _ = q.shape
    nb = S // rows
    final = extra is not None
    blk = (lambda i: nb - 1 - i) if reverse else (lambda i: i)
    la_blk = 1 if reverse else 0
    qk_spec = pl.BlockSpec((None, rows, QK_A), lambda b, i: (b, blk(i), 0))
    v_spec = pl.BlockSpec((None, rows, A_WIDTH), lambda b, i: (b, blk(i), 0))
    la_spec = pl.BlockSpec((None, rows, QK_A), lambda b, i: (b, blk(i), la_blk))
    in_specs = [qk_spec, qk_spec, v_spec, la_spec]
    args = [q, k, v, la]
    if final:
        ob, sz, g = extra
        in_specs += [v_spec, v_spec, _const_spec((1, A_WIDTH))]
        args += [ob, sz, g]
    return pl.pallas_call(
        functools.partial(_gla_kernel, reverse=reverse, final=final, n_super=rows // GLA_SUPER),
        grid=(B, nb),
        in_specs=in_specs,
        out_specs=v_spec,
        out_shape=jax.ShapeDtypeStruct((B, S, A_WIDTH), BF16 if final else F32),
        scratch_shapes=[pltpu.VMEM((GLA_HEADS, GLA_DK, GLA_DV), F32)],
        compiler_params=pltpu.CompilerParams(
            dimension_semantics=("parallel", "arbitrary"), vmem_limit_bytes=VMEM_LIMIT),
        name="gla_fwd" if final else "gla_bwd",
    )(*args)


def _attn_kernel(q_ref, k_ref, vt_ref, sz_ref, o_ref, qt_ref, s_ref, mx_ref, acc_ref, m_ref, *, tk):
    S = k_ref.shape[0]
    tq = q_ref.shape[0]
    nk = S // tk
    for hh in range(ATT_GROUP):
        qt_ref[:, hh * tq:(hh + 1) * tq] = q_ref[:, hh * ATT_HD:(hh + 1) * ATT_HD].astype(F32).T.astype(BF16)
    acc_ref[...] = jnp.zeros_like(acc_ref)
    m_ref[...] = jnp.full_like(m_ref, NEG_BIG)

    def scores(j, slot):
        off = pl.multiple_of(j * tk, tk)
        s = jnp.dot(k_ref[pl.ds(off, tk), :], qt_ref[...], preferred_element_type=F32)
        s_ref[slot] = s
        mx_ref[slot] = jnp.max(s, axis=0, keepdims=True)

    def accumulate(j, slot):
        off = pl.multiple_of(j * tk, tk)
        m_old = m_ref[...]
        m_new = jnp.maximum(m_old, mx_ref[slot])
        p = jnp.exp2(s_ref[slot] - m_new).astype(BF16)
        pv = jnp.dot(vt_ref[:, pl.ds(off, tk)], p, preferred_element_type=F32)
        acc_ref[...] = acc_ref[...] * jnp.exp2(m_old - m_new) + pv
        m_ref[...] = m_new

    scores(0, 0)

    unroll = min(ATT_UNROLL, nk)

    def group(jj, carry):
        j = unroll * jj
        for u in range(unroll):
            scores(jnp.minimum(j + u + 1, nk - 1), (u + 1) % 2)
            accumulate(j + u, u % 2)
        return carry

    lax.fori_loop(0, nk // unroll, group, 0)

    acc = acc_ref[...]
    o_t = acc[:ATT_HD, :] * (1.0 / acc[ATT_HD:ATT_HD + 1, :])
    for hh in range(ATT_GROUP):
        sl = slice(hh * ATT_HD, (hh + 1) * ATT_HD)
        o_ref[:, sl] = (o_t[:, hh * tq:(hh + 1) * tq].T * sz_ref[:, sl].astype(F32)).astype(o_ref.dtype)


def _attention(qb, kb, vt, szb, *, tq, tk):
    B, S, _ = qb.shape
    nk = S // tk
    assert S % tq == 0 and nk % 2 == 0 and nk % min(ATT_UNROLL, nk) == 0
    gw = ATT_GROUP * ATT_HD
    q_spec = pl.BlockSpec((None, tq, gw), lambda b, kh, i: (b, i, kh))
    return pl.pallas_call(
        functools.partial(_attn_kernel, tk=tk),
        grid=(B, ATT_KV_HEADS, S // tq),
        in_specs=[
            q_spec,
            pl.BlockSpec((None, S, ATT_HD), lambda b, kh, i: (b, 0, kh)),
            pl.BlockSpec((None, VT_ROWS, S), lambda b, kh, i: (b, kh, 0)),
            q_spec,
        ],
        out_specs=q_spec,
        out_shape=jax.ShapeDtypeStruct((B, S, B_WIDTH), BF16),
        scratch_shapes=[
            pltpu.VMEM((ATT_HD, ATT_GROUP * tq), BF16),
            pltpu.VMEM((2, tk, ATT_GROUP * tq), F32),
            pltpu.VMEM((2, 1, ATT_GROUP * tq), F32),
            pltpu.VMEM((VT_ROWS, ATT_GROUP * tq), F32),
            pltpu.VMEM((1, ATT_GROUP * tq), F32),
        ],
        compiler_params=pltpu.CompilerParams(
            dimension_semantics=("parallel", "parallel", "arbitrary"), vmem_limit_bytes=VMEM_LIMIT),
        name="attn",
    )(qb, kb, vt, szb)


def _ev_out_kernel(x_ref, ma_ref, mb_ref, w_ref, o_ref):
    acc = jnp.dot(ma_ref[...], w_ref[:A_WIDTH, :], preferred_element_type=F32)
    acc = acc + jnp.dot(mb_ref[...], w_ref[A_WIDTH:, :], preferred_element_type=F32)
    o_ref[...] = x_ref[...] + acc


def _ev_out(x, ma, mb, w, *, tm):
    B, S, D = x.shape
    row = lambda width: pl.BlockSpec((None, tm, width), lambda b, i: (b, i, 0))
    return pl.pallas_call(
        _ev_out_kernel,
        grid=(B, S // tm),
        in_specs=[row(D), row(A_WIDTH), row(B_WIDTH), _const_spec((A_WIDTH + B_WIDTH, D))],
        out_specs=row(D),
        out_shape=jax.ShapeDtypeStruct((B, S, D), F32),
        compiler_params=pltpu.CompilerParams(
            dimension_semantics=("parallel", "parallel"), vmem_limit_bytes=VMEM_LIMIT),
        name="ev_out",
    )(x, ma, mb, w)


def _odd_kernel(*refs, seq_len, final):
    if final:
        xp_ref, x_ref, xn_ref, ng_ref, win_ref, pw_ref, pb_ref, ps_ref, wout_ref, fg_ref, o_ref = refs
    else:
        xp_ref, x_ref, xn_ref, ng_ref, win_ref, pw_ref, pb_ref, ps_ref, wout_ref, o_ref = refs
    tm = x_ref.shape[0]
    H = POOL_HALO
    t0 = pl.program_id(1) * tm
    x = x_ref[...]
    xe = jnp.concatenate([x, xp_ref[...], xn_ref[...]], axis=0)
    h = _rms(xe, ng_ref[...]).astype(BF16)
    ue = jnp.dot(h, win_ref[:, :POOL_WIDTH], preferred_element_type=F32)
    z = jnp.dot(h[:tm], win_ref[:, POOL_WIDTH:], preferred_element_type=F32)
    ue_bf = ue.astype(BF16)

    t = t0 + lax.broadcasted_iota(jnp.int32, (tm, 1), 0)
    c = lax.broadcasted_iota(jnp.int32, (1, tm + 2 * H), 1)
    pos = t0 + jnp.where(c < tm, c, jnp.where(c < tm + H, c - tm - H, c - H))
    valid = (pos >= 0) & (pos < seq_len)

    ms = []
    for g, w in enumerate(POOL_WINDOWS):
        lo = t - w // 2
        hi = lo + w
        band = ((pos >= lo) & (pos < hi) & valid).astype(BF16)
        cnt = (jnp.minimum(hi, seq_len) - jnp.maximum(lo, 0)).astype(F32)
        cols = slice(g * POOL_GW, (g + 1) * POOL_GW)
        pooled = jnp.dot(band, ue_bf[:, cols], preferred_element_type=F32) / cnt
        d = (pooled - ue[:tm, cols]).astype(BF16)
        ms.append(jnp.dot(d, pw_ref[g], preferred_element_type=F32))
    m = jnp.concatenate(ms, axis=1) + pb_ref[...]
    m = (m * ps_ref[...] * _silu(z)).astype(BF16)
    y = x + jnp.dot(m, wout_ref[...], preferred_element_type=F32)
    if final:
        y = _rms(y, fg_ref[...])
    o_ref[...] = y


def _odd(x, ng, win, pw, pb, ps, wout, fg, *, tm):
    B, S, D = x.shape
    H = POOL_HALO
    nh = tm // H
    last = S // H - 1
    row = pl.BlockSpec((None, tm, D), lambda b, i: (b, i, 0))
    prev = pl.BlockSpec((None, H, D), lambda b, i: (b, jnp.maximum(i * nh - 1, 0), 0))
    nxt = pl.BlockSpec((None, H, D), lambda b, i: (b, jnp.minimum((i + 1) * nh, last), 0))
    final = fg is not None
    in_specs = [
        prev, row, nxt, _const_spec((1, D)), _const_spec((D, 2 * POOL_WIDTH)),
        _const_spec((POOL_GROUPS, POOL_GW, POOL_GW)), _const_spec((1, POOL_WIDTH)),
        _const_spec((1, POOL_WIDTH)), _const_spec((POOL_WIDTH, D)),
    ]
    args = [x, x, x, ng, win, pw, pb, ps, wout]
    if final:
        in_specs.append(_const_spec((1, D)))
        args.append(fg)
    return pl.pallas_call(
        functools.partial(_odd_kernel, seq_len=S, final=final),
        grid=(B, S // tm),
        in_specs=in_specs,
        out_specs=row,
        out_shape=jax.ShapeDtypeStruct((B, S, D), F32),
        compiler_params=pltpu.CompilerParams(
            dimension_semantics=("parallel", "parallel"), vmem_limit_bytes=VMEM_LIMIT),
        name="odd_final" if final else "odd",
    )(*args)


def _rope_tables(S):
    rows = S // GRID_W
    row = jnp.repeat(jnp.arange(rows, dtype=F32), GRID_W)
    col = jnp.tile(jnp.arange(GRID_W, dtype=F32), rows)
    n_pairs = ATT_HD // 4
    freqs = ROPE_THETA ** (-jnp.arange(n_pairs, dtype=F32) / n_pairs)
    ang = jnp.concatenate([row[:, None] * freqs, col[:, None] * freqs], axis=-1)
    c, s = jnp.cos(ang), jnp.sin(ang)
    return jnp.concatenate([c, c], axis=-1), jnp.concatenate([-s, s], axis=-1)


_PAIR_PERM = np.concatenate([np.arange(0, ATT_HD, 2), np.arange(1, ATT_HD, 2)])


def _even_weights(w_in, gate_w, gate_b, q_norm_g, k_norm_g):
    offs = np.concatenate([[0], np.cumsum(EV_SPLITS)])
    seg = lambda i: w_in[:, offs[i]:offs[i + 1]]
    q_a, k_a, v_a, lr_f, lr_b, z_a, q_b, k_b, v_b, z_b = [seg(i) for i in range(10)]
    D = w_in.shape[0]

    def perm_heads(w, nh):
        return w.reshape(D, nh, ATT_HD)[:, :, _PAIR_PERM].reshape(D, nh * ATT_HD)

    lr = jnp.concatenate([lr_f, lr_b, jnp.zeros((D, LR_PAD - 2 * GLA_LOWRANK), w_in.dtype)], axis=1)
    w = jnp.concatenate(
        [q_a, k_a, v_a, z_a, perm_heads(q_b, ATT_HEADS), perm_heads(k_b, ATT_KV_HEADS), v_b, z_b, lr], axis=1)
    gw = jnp.zeros((LR_PAD, 2 * QK_A), F32)
    gw = gw.at[:GLA_LOWRANK, :QK_A].set(gate_w[0])
    gw = gw.at[GLA_LOWRANK:2 * GLA_LOWRANK, QK_A:].set(gate_w[1])
    gb = jnp.concatenate([gate_b[0], gate_b[1]])[None, :]
    return (w.astype(BF16), gw.astype(BF16), gb.astype(F32),
            q_norm_g[_PAIR_PERM][None, :], k_norm_g[_PAIR_PERM][None, :])


def _trunk(x, norm_g, final_norm_g, ev_w_in, ev_gla_gate_w, ev_gla_gate_b, ev_gla_norm_g,
           ev_q_norm_g, ev_k_norm_g, ev_w_out, od_w_in, od_pool_w, od_pool_b, od_pool_scale, od_w_out):
    B, S, D = x.shape
    cos2, sin2 = _rope_tables(S)
    tm_in = min(256, S)
    tm_out = min(512, S)
    tm_odd = min(256, S)
    gla_rows = min(512, S)
    tq = min(256, S)
    tk = min(512, S)
    for layer in range(DEPTH):
        i = layer // 2
        ng = norm_g[layer][None, :]
        if layer % 2 == 0:
            w, gw, gb, qg, kg = _even_weights(ev_w_in[i], ev_gla_gate_w[i], ev_gla_gate_b[i],
                                              ev_q_norm_g[i], ev_k_norm_g[i])
            qa, ka, va, la, sza, qb, kb, vt, szb = _ev_in(x, ng, w, gw, gb, qg, kg, cos2, sin2, tm=tm_in)
            o_bwd = _gla(qa, ka, va, la, reverse=True, rows=gla_rows)
            ma = _gla(qa, ka, va, la, reverse=False, rows=gla_rows,
                      extra=(o_bwd, sza, ev_gla_norm_g[i][None, :]))
            mb = _attention(qb, kb, vt, szb, tq=tq, tk=tk)
            x = _ev_out(x, ma, mb, ev_w_out[i].astype(BF16), tm=tm_out)
        else:
            fg = final_norm_g[None, :] if layer == DEPTH - 1 else None
            x = _odd(x, ng, od_w_in[i].astype(BF16), od_pool_w[i].astype(BF16), od_pool_b[i][None, :],
                     od_pool_scale[i][None, :], od_w_out[i].astype(BF16), fg, tm=tm_odd)
    return x


def kernel(x_prompt, x_sample, norm_g, final_norm_g, ev_w_in, ev_gla_gate_w, ev_gla_gate_b,
           ev_gla_norm_g, ev_q_norm_g, ev_k_norm_g, ev_w_out, od_w_in, od_pool_w, od_pool_b,
           od_pool_scale, od_w_out):
    nb = x_prompt.shape[0]
    x = jnp.concatenate([x_prompt, x_sample], axis=0)
    y = _trunk(x, norm_g, final_norm_g, ev_w_in, ev_gla_gate_w, ev_gla_gate_b, ev_gla_norm_g,
               ev_q_norm_g, ev_k_norm_g, ev_w_out, od_w_in, od_pool_w, od_pool_b, od_pool_scale, od_w_out)
    return (y[:nb], y[nb:])
```

```python
import functools

import numpy as np
import jax
import jax.numpy as jnp
from jax import lax
from jax.experimental import pallas as pl
from jax.experimental.pallas import tpu as pltpu

F32 = jnp.float32
BF16 = jnp.bfloat16

D_MODEL = 1024
DEPTH = 4
GRID_W = 64
EPS = 1e-6

GLA_HEADS = 4
GLA_DK = 128
GLA_DV = 256
GLA_LOWRANK = 16
GLA_TAU = 16.0
GLA_CHUNK = 64
GLA_SUPER = 256

ATT_HEADS = 8
ATT_KV_HEADS = 2
ATT_HD = 128
ATT_GROUP = ATT_HEADS // ATT_KV_HEADS
ROPE_THETA = 10000.0
ATT_BOUND_LIMIT = 60.0
ATT_UNROLL = 8

POOL_WINDOWS = (2, 4, 8, 16)
POOL_GROUPS = 4
POOL_WIDTH = 2 * D_MODEL
POOL_GW = POOL_WIDTH // POOL_GROUPS
POOL_HALO = 8

A_WIDTH = GLA_HEADS * GLA_DV
B_WIDTH = ATT_HEADS * ATT_HD
QK_A = GLA_HEADS * GLA_DK
KV_B = ATT_KV_HEADS * ATT_HD
EV_SPLITS = (QK_A, QK_A, A_WIDTH, GLA_LOWRANK, GLA_LOWRANK, A_WIDTH, B_WIDTH, KV_B, KV_B, B_WIDTH)

C_QA = 0
C_KA = C_QA + QK_A
C_VA = C_KA + QK_A
C_ZA = C_VA + A_WIDTH
C_QB = C_ZA + A_WIDTH
C_KB = C_QB + B_WIDTH
C_VB = C_KB + KV_B
C_ZB = C_VB + KV_B
C_LR = C_ZB + B_WIDTH
LR_PAD = 128
EV_COLS = C_LR + LR_PAD

LOG2E = 1.4426950408889634
NEG_BIG = -1e30

VMEM_LIMIT = 56 * 1024 * 1024


def _rms(x, g):
    return x * lax.rsqrt(jnp.mean(x * x, axis=-1, keepdims=True) + EPS) * g


def _silu(z):
    return z * jax.nn.sigmoid(z)


def _log_sigmoid(g):
    return jnp.minimum(g, 0.0) - jnp.log(1.0 + jnp.exp(-jnp.abs(g)))


def _const_spec(shape):
    nd = len(shape)
    return pl.BlockSpec(shape, lambda *_: (0,) * nd)


def _ev_in_kernel(x_ref, ng_ref, w_ref, gw_ref, gb_ref, qg_ref, kg_ref, cos_ref, sin_ref,
                  qa_ref, ka_ref, va_ref, la_ref, sza_ref, qb_ref, kb_ref, vt_ref, szb_ref):
    h = _rms(x_ref[...], ng_ref[...]).astype(BF16)

    def proj(c0, width):
        return jnp.dot(h, w_ref[:, c0:c0 + width], preferred_element_type=F32)

    lr = proj(C_LR, LR_PAD).astype(BF16)
    gate = jnp.dot(lr, gw_ref[...], preferred_element_type=F32) + gb_ref[...]
    la_ref[...] = _log_sigmoid(gate) * (1.0 / GLA_TAU)

    cos = cos_ref[...]
    sin = sin_ref[...]

    def norm_rope(xh, g):
        xn = _rms(xh, g)
        return xn * cos + pltpu.roll(xn, ATT_HD // 2, 1) * sin

    qscale = (ATT_HD ** -0.5) * LOG2E
    qb = proj(C_QB, B_WIDTH)
    for hh in range(ATT_HEADS):
        sl = slice(hh * ATT_HD, (hh + 1) * ATT_HD)
        qb_ref[:, sl] = (norm_rope(qb[:, sl], qg_ref[...]) * qscale).astype(BF16)
    kb = proj(C_KB, KV_B)
    for hh in range(ATT_KV_HEADS):
        sl = slice(hh * ATT_HD, (hh + 1) * ATT_HD)
        kb_ref[:, sl] = norm_rope(kb[:, sl], kg_ref[...]).astype(BF16)

    vt_ref[...] = proj(C_VB, KV_B).T.astype(BF16)

    sza_ref[...] = _silu(proj(C_ZA, A_WIDTH)).astype(BF16)
    szb_ref[...] = _silu(proj(C_ZB, B_WIDTH)).astype(BF16)
    qa_ref[...] = (proj(C_QA, QK_A) * (GLA_DK ** -0.5)).astype(BF16)
    ka_ref[...] = proj(C_KA, QK_A).astype(BF16)
    va_ref[...] = proj(C_VA, A_WIDTH).astype(BF16)


def _ev_in(x, ng, w, gw, gb, qg, kg, cos2, sin2, *, tm):
    B, S, D = x.shape
    nt = S // tm
    row = lambda width: pl.BlockSpec((None, tm, width), lambda b, i: (b, i, 0))
    outs = [
        (QK_A, BF16), (QK_A, BF16), (A_WIDTH, BF16), (2 * QK_A, F32), (A_WIDTH, BF16),
        (B_WIDTH, BF16), (KV_B, BF16), None, (B_WIDTH, BF16),
    ]
    out_shape, out_specs = [], []
    for o in outs:
        if o is None:
            out_shape.append(jax.ShapeDtypeStruct((B, KV_B, S), BF16))
            out_specs.append(pl.BlockSpec((None, KV_B, tm), lambda b, i: (b, 0, i)))
        else:
            out_shape.append(jax.ShapeDtypeStruct((B, S, o[0]), o[1]))
            out_specs.append(row(o[0]))
    return pl.pallas_call(
        _ev_in_kernel,
        grid=(B, nt),
        in_specs=[
            row(D), _const_spec((1, D)), _const_spec((D, EV_COLS)), _const_spec((LR_PAD, 2 * QK_A)),
            _const_spec((1, 2 * QK_A)), _const_spec((1, ATT_HD)), _const_spec((1, ATT_HD)),
            pl.BlockSpec((tm, ATT_HD), lambda b, i: (i, 0)), pl.BlockSpec((tm, ATT_HD), lambda b, i: (i, 0)),
        ],
        out_specs=out_specs,
        out_shape=out_shape,
        compiler_params=pltpu.CompilerParams(
            dimension_semantics=("parallel", "parallel"), vmem_limit_bytes=VMEM_LIMIT),
        name="ev_in",
    )(x, ng, w, gw, gb, qg, kg, cos2, sin2)


def _split_hi_lo(a):
    hi = a.astype(BF16)
    lo = (a - hi.astype(F32)).astype(BF16)
    return hi, lo


def _gla_kernel(*refs, reverse, final, n_super):
    if final:
        q_ref, k_ref, v_ref, la_ref, ob_ref, sz_ref, g_ref, o_ref, st_ref = refs
    else:
        q_ref, k_ref, v_ref, la_ref, o_ref, st_ref = refs
    SC, C = GLA_SUPER, GLA_CHUNK
    n_chunk = SC // C

    @pl.when(pl.program_id(1) == 0)
    def _():
        st_ref[...] = jnp.zeros_like(st_ref)

    ri = lax.broadcasted_iota(jnp.int32, (SC, SC), 0)
    ci = lax.broadcasted_iota(jnp.int32, (SC, SC), 1)
    same = (ri // C) == (ci // C)
    if reverse:
        incl = same & (ci >= ri)
        incl_t = same & (ri >= ci)
        strict_t = same & (ri < ci)
    else:
        incl = same & (ci <= ri)
        incl_t = same & (ri <= ci)
        strict_t = same & (ri > ci)
    incl_bf = incl.astype(BF16)
    incl_t_bf = incl_t.astype(BF16)
    strict_t_bf = strict_t.astype(BF16)
    lane_chunk = lax.broadcasted_iota(jnp.int32, (GLA_DK, SC), 1) // C

    order = range(n_super - 1, -1, -1) if reverse else range(n_super)
    corder = range(n_chunk - 1, -1, -1) if reverse else range(n_chunk)
    bodies = [(sc, hd) for sc in order for hd in range(GLA_HEADS)]
    rows = {sc: slice(sc * SC, (sc + 1) * SC) for sc in order}
    kcol = [slice(hd * GLA_DK, (hd + 1) * GLA_DK) for hd in range(GLA_HEADS)]
    vcol = [slice(hd * GLA_DV, (hd + 1) * GLA_DV) for hd in range(GLA_HEADS)]
    dot = functools.partial(jnp.dot, preferred_element_type=F32)

    hl, hl_t, k_t = {}, {}, {}
    for key in bodies:
        sc, hd = key
        la = la_ref[rows[sc], kcol[hd]]
        hi, lo = _split_hi_lo(la)
        hl[key] = jnp.concatenate([hi, lo], axis=1)
        hi_t, lo_t = _split_hi_lo(la.T)
        hl_t[key] = jnp.concatenate([hi_t, lo_t], axis=0)
        k_t[key] = k_ref[rows[sc], kcol[hd]].astype(F32).T
    b2 = {key: dot(incl_bf, hl[key]) for key in bodies}
    bt2 = {key: dot(hl_t[key], incl_t_bf) for key in bodies}
    gt2 = {key: dot(hl_t[key], strict_t_bf) for key in bodies}

    q_dec, k_inv_t, k_end_t, tot_t = {}, {}, {}, {}
    for key in bodies:
        sc, hd = key
        b = b2[key][:, :GLA_DK] + b2[key][:, GLA_DK:]
        b_t = bt2[key][:GLA_DK] + bt2[key][GLA_DK:]
        g_t = gt2[key][:GLA_DK] + gt2[key][GLA_DK:]
        tot_t[key] = b_t + g_t
        q_dec[key] = (q_ref[rows[sc], kcol[hd]].astype(F32) * jnp.exp(b)).astype(BF16)
        k_inv_t[key] = (k_t[key] * jnp.exp(-b_t)).astype(BF16)
        k_end_t[key] = k_t[key] * jnp.exp(g_t)

    scores = {key: dot(q_dec[key], k_inv_t[key]) for key in bodies}
    o_sc = {}
    for key in bodies:
        sc, hd = key
        o_sc[key] = dot(jnp.where(incl, scores[key], 0.0).astype(BF16), v_ref[rows[sc], vcol[hd]])
    kv = {}
    for key in bodies:
        sc, hd = key
        for c in corder:
            k_end_c = jnp.where(lane_chunk == c, k_end_t[key], 0.0).astype(BF16)
            kv[key, c] = dot(k_end_c, v_ref[rows[sc], vcol[hd]])

    inter = {}
    for sc in order:
        for c in corder:
            for hd in range(GLA_HEADS):
                key = (sc, hd)
                state = st_ref[hd]
                inter[key, c] = dot(q_dec[key][c * C:(c + 1) * C, :], state.astype(BF16))
                st_ref[hd] = state * jnp.exp(tot_t[key][:, c * C:c * C + 1]) + kv[key, c]

    for key in bodies:
        sc, hd = key
        o = o_sc[key] + jnp.concatenate([inter[key, c] for c in range(n_chunk)], axis=0)
        if final:
            o = o + ob_ref[rows[sc], vcol[hd]]
            o = _rms(o, g_ref[:, vcol[hd]]) * sz_ref[rows[sc], vcol[hd]].astype(F32)
        o_ref[rows[sc], vcol[hd]] = o.astype(o_ref.dtype)


def _gla(q, k, v, la, *, reverse, rows, extra=None):
    B, S, _ = q.shape
    nb = S // rows
    final = extra is not None
    blk = (lambda i: nb - 1 - i) if reverse else (lambda i: i)
    la_blk = 1 if reverse else 0
    qk_spec = pl.BlockSpec((None, rows, QK_A), lambda b, i: (b, blk(i), 0))
    v_spec = pl.BlockSpec((None, rows, A_WIDTH), lambda b, i: (b, blk(i), 0))
    la_spec = pl.BlockSpec((None, rows, QK_A), lambda b, i: (b, blk(i), la_blk))
    in_specs = [qk_spec, qk_spec, v_spec, la_spec]
    args = [q, k, v, la]
    if final:
        ob, sz, g = extra
        in_specs += [v_spec, v_spec, _const_spec((1, A_WIDTH))]
        args += [ob, sz, g]
    return pl.pallas_call(
        functools.partial(_gla_kernel, reverse=reverse, final=final, n_super=rows // GLA_SUPER),
        grid=(B, nb),
        in_specs=in_specs,
        out_specs=v_spec,
        out_shape=jax.ShapeDtypeStruct((B, S, A_WIDTH), BF16 if final else F32),
        scratch_shapes=[pltpu.VMEM((GLA_HEADS, GLA_DK, GLA_DV), F32)],
        compiler_params=pltpu.CompilerParams(
            dimension_semantics=("parallel", "arbitrary"), vmem_limit_bytes=VMEM_LIMIT),
        name="gla_fwd" if final else "gla_bwd",
    )(*args)


def _load_qt(q_ref, qt_ref, tq):
    for hh in range(ATT_GROUP):
        qt_ref[:, hh * tq:(hh + 1) * tq] = q_ref[:, hh * ATT_HD:(hh + 1) * ATT_HD].astype(F32).T.astype(BF16)


def _attn_finish(acc_ref, l_ref, sz_ref, o_ref, tq):
    o_t = acc_ref[...] * (1.0 / jnp.sum(l_ref[...], axis=0, keepdims=True))
    for hh in range(ATT_GROUP):
        sl = slice(hh * ATT_HD, (hh + 1) * ATT_HD)
        o_ref[:, sl] = (o_t[:, hh * tq:(hh + 1) * tq].T * sz_ref[:, sl].astype(F32)).astype(o_ref.dtype)


def _sublane_partial_sums(p):
    return p.reshape(p.shape[0] // 8, 8, p.shape[1]).sum(axis=0)


def _attn_bounded_kernel(q_ref, k_ref, vt_ref, sz_ref, o_ref, qt_ref, p_ref, acc_ref, l_ref, *, tk):
    S = k_ref.shape[0]
    tq = q_ref.shape[0]
    nk = S // tk
    _load_qt(q_ref, qt_ref, tq)
    acc_ref[...] = jnp.zeros_like(acc_ref)
    l_ref[...] = jnp.zeros_like(l_ref)
    p_ref[1] = jnp.zeros_like(p_ref[1])

    def probs(j, slot):
        off = pl.multiple_of(j * tk, tk)
        s = jnp.dot(k_ref[pl.ds(off, tk), :], qt_ref[...], preferred_element_type=F32)
        p = jnp.exp2(s)
        l_ref[...] += _sublane_partial_sums(p)
        p_ref[slot] = p.astype(BF16)

    def accumulate(j, slot):
        off = pl.multiple_of(j * tk, tk)
        acc_ref[...] += jnp.dot(vt_ref[:, pl.ds(off, tk)], p_ref[slot], preferred_element_type=F32)

    unroll = min(ATT_UNROLL, nk)

    def group(jj, carry):
        j = unroll * jj
        for u in range(unroll):
            probs(j + u, u % 2)
            accumulate(jnp.maximum(j + u - 1, 0), (u + 1) % 2)
        return carry

    lax.fori_loop(0, nk // unroll, group, 0)
    accumulate(nk - 1, (nk - 1) % 2)
    _attn_finish(acc_ref, l_ref, sz_ref, o_ref, tq)


def _attn_online_kernel(q_ref, k_ref, vt_ref, sz_ref, o_ref, qt_ref, s_ref, mx_ref, acc_ref, l_ref, m_ref, *, tk):
    S = k_ref.shape[0]
    tq = q_ref.shape[0]
    nk = S // tk
    _load_qt(q_ref, qt_ref, tq)
    acc_ref[...] = jnp.zeros_like(acc_ref)
    l_ref[...] = jnp.zeros_like(l_ref)
    m_ref[...] = jnp.full_like(m_ref, NEG_BIG)

    def scores(j, slot):
        off = pl.multiple_of(j * tk, tk)
        s = jnp.dot(k_ref[pl.ds(off, tk), :], qt_ref[...], preferred_element_type=F32)
        s_ref[slot] = s
        mx_ref[slot] = jnp.max(s, axis=0, keepdims=True)

    def accumulate(j, slot):
        off = pl.multiple_of(j * tk, tk)
        m_old = m_ref[...]
        m_new = jnp.maximum(m_old, mx_ref[slot])
        alpha = jnp.exp2(m_old - m_new)
        p = jnp.exp2(s_ref[slot] - m_new)
        l_ref[...] = l_ref[...] * alpha + _sublane_partial_sums(p)
        pv = jnp.dot(vt_ref[:, pl.ds(off, tk)], p.astype(BF16), preferred_element_type=F32)
        acc_ref[...] = acc_ref[...] * alpha + pv
        m_ref[...] = m_new

    scores(0, 0)

    def pair(jj, carry):
        j = 2 * jj
        scores(j + 1, 1)
        accumulate(j, 0)
        scores(jnp.minimum(j + 2, nk - 1), 0)
        accumulate(j + 1, 1)
        return carry

    lax.fori_loop(0, nk // 2, pair, 0)
    _attn_finish(acc_ref, l_ref, sz_ref, o_ref, tq)


def _attention(qb, kb, vt, szb, score_bound, *, tq, tk):
    B, S, _ = qb.shape
    nk = S // tk
    assert S % tq == 0 and nk % 2 == 0 and nk % min(ATT_UNROLL, nk) == 0
    gw = ATT_GROUP * ATT_HD
    n = ATT_GROUP * tq

    def call(body, scratch, name):
        q_spec = pl.BlockSpec((None, tq, gw), lambda b, kh, i: (b, i, kh))
        return pl.pallas_call(
            functools.partial(body, tk=tk),
            grid=(B, ATT_KV_HEADS, S // tq),
            in_specs=[
                q_spec,
                pl.BlockSpec((None, S, ATT_HD), lambda b, kh, i: (b, 0, kh)),
                pl.BlockSpec((None, ATT_HD, S), lambda b, kh, i: (b, kh, 0)),
                q_spec,
            ],
            out_specs=q_spec,
            out_shape=jax.ShapeDtypeStruct((B, S, B_WIDTH), BF16),
            scratch_shapes=scratch,
            compiler_params=pltpu.CompilerParams(
                dimension_semantics=("parallel", "parallel", "arbitrary"), vmem_limit_bytes=VMEM_LIMIT),
            name=name)

    bounded = call(_attn_bounded_kernel, [
        pltpu.VMEM((ATT_HD, n), BF16), pltpu.VMEM((2, tk, n), BF16),
        pltpu.VMEM((ATT_HD, n), F32), pltpu.VMEM((8, n), F32)], "attn_bounded")
    online = call(_attn_online_kernel, [
        pltpu.VMEM((ATT_HD, n), BF16), pltpu.VMEM((2, tk, n), F32), pltpu.VMEM((2, 1, n), F32),
        pltpu.VMEM((ATT_HD, n), F32), pltpu.VMEM((8, n), F32), pltpu.VMEM((1, n), F32)], "attn_online")
    return lax.cond(score_bound < ATT_BOUND_LIMIT, bounded, online, qb, kb, vt, szb)


def _ev_out_kernel(x_ref, ma_ref, mb_ref, w_ref, o_ref):
    acc = jnp.dot(ma_ref[...], w_ref[:A_WIDTH, :], preferred_element_type=F32)
    acc = acc + jnp.dot(mb_ref[...], w_ref[A_WIDTH:, :], preferred_element_type=F32)
    o_ref[...] = x_ref[...] + acc


def _ev_out(x, ma, mb, w, *, tm):
    B, S, D = x.shape
    row = lambda width: pl.BlockSpec((None, tm, width), lambda b, i: (b, i, 0))
    return pl.pallas_call(
        _ev_out_kernel,
        grid=(B, S // tm),
        in_specs=[row(D), row(A_WIDTH), row(B_WIDTH), _const_spec((A_WIDTH + B_WIDTH, D))],
        out_specs=row(D),
        out_shape=jax.ShapeDtypeStruct((B, S, D), F32),
        compiler_params=pltpu.CompilerParams(
            dimension_semantics=("parallel", "parallel"), vmem_limit_bytes=VMEM_LIMIT),
        name="ev_out",
    )(x, ma, mb, w)


def _odd_kernel(*refs, seq_len, final):
    if final:
        xp_ref, x_ref, xn_ref, ng_ref, win_ref, pw_ref, pb_ref, ps_ref, wout_ref, fg_ref, o_ref = refs
    else:
        xp_ref, x_ref, xn_ref, ng_ref, win_ref, pw_ref, pb_ref, ps_ref, wout_ref, o_ref = refs
    tm = x_ref.shape[0]
    H = POOL_HALO
    t0 = pl.program_id(1) * tm
    x = x_ref[...]
    xe = jnp.concatenate([x, xp_ref[...], xn_ref[...]], axis=0)
    h = _rms(xe, ng_ref[...]).astype(BF16)
    ue = jnp.dot(h, win_ref[:, :POOL_WIDTH], preferred_element_type=F32)
    z = jnp.dot(h[:tm], win_ref[:, POOL_WIDTH:], preferred_element_type=F32)
    ue_bf = ue.astype(BF16)

    t = t0 + lax.broadcasted_iota(jnp.int32, (tm, 1), 0)
    c = lax.broadcasted_iota(jnp.int32, (1, tm + 2 * H), 1)
    pos = t0 + jnp.where(c < tm, c, jnp.where(c < tm + H, c - tm - H, c - H))
    valid = (pos >= 0) & (pos < seq_len)

    ms = []
    for g, w in enumerate(POOL_WINDOWS):
        lo = t - w // 2
        hi = lo + w
        band = ((pos >= lo) & (pos < hi) & valid).astype(BF16)
        cnt = (jnp.minimum(hi, seq_len) - jnp.maximum(lo, 0)).astype(F32)
        cols = slice(g * POOL_GW, (g + 1) * POOL_GW)
        pooled = jnp.dot(band, ue_bf[:, cols], preferred_element_type=F32) / cnt
        d = (pooled - ue[:tm, cols]).astype(BF16)
        ms.append(jnp.dot(d, pw_ref[g], preferred_element_type=F32))
    m = jnp.concatenate(ms, axis=1) + pb_ref[...]
    m = (m * ps_ref[...] * _silu(z)).astype(BF16)
    y = x + jnp.dot(m, wout_ref[...], preferred_element_type=F32)
    if final:
        y = _rms(y, fg_ref[...])
    o_ref[...] = y


def _odd(x, ng, win, pw, pb, ps, wout, fg, *, tm):
    B, S, D = x.shape
    H = POOL_HALO
    nh = tm // H
    last = S // H - 1
    row = pl.BlockSpec((None, tm, D), lambda b, i: (b, i, 0))
    prev = pl.BlockSpec((None, H, D), lambda b, i: (b, jnp.maximum(i * nh - 1, 0), 0))
    nxt = pl.BlockSpec((None, H, D), lambda b, i: (b, jnp.minimum((i + 1) * nh, last), 0))
    final = fg is not None
    in_specs = [
        prev, row, nxt, _const_spec((1, D)), _const_spec((D, 2 * POOL_WIDTH)),
        _const_spec((POOL_GROUPS, POOL_GW, POOL_GW)), _const_spec((1, POOL_WIDTH)),
        _const_spec((1, POOL_WIDTH)), _const_spec((POOL_WIDTH, D)),
    ]
    args = [x, x, x, ng, win, pw, pb, ps, wout]
    if final:
        in_specs.append(_const_spec((1, D)))
        args.append(fg)
    return pl.pallas_call(
        functools.partial(_odd_kernel, seq_len=S, final=final),
        grid=(B, S // tm),
        in_specs=in_specs,
        out_specs=row,
        out_shape=jax.ShapeDtypeStruct((B, S, D), F32),
        compiler_params=pltpu.CompilerParams(
            dimension_semantics=("parallel", "parallel"), vmem_limit_bytes=VMEM_LIMIT),
        name="odd_final" if final else "odd",
    )(*args)


def _rope_tables(S):
    rows = S // GRID_W
    row = jnp.repeat(jnp.arange(rows, dtype=F32), GRID_W)
    col = jnp.tile(jnp.arange(GRID_W, dtype=F32), rows)
    n_pairs = ATT_HD // 4
    freqs = ROPE_THETA ** (-jnp.arange(n_pairs, dtype=F32) / n_pairs)
    ang = jnp.concatenate([row[:, None] * freqs, col[:, None] * freqs], axis=-1)
    c, s = jnp.cos(ang), jnp.sin(ang)
    return jnp.concatenate([c, c], axis=-1), jnp.concatenate([-s, s], axis=-1)


_PAIR_PERM = np.concatenate([np.arange(0, ATT_HD, 2), np.arange(1, ATT_HD, 2)])


def _even_weights(w_in, gate_w, gate_b, q_norm_g, k_norm_g):
    offs = np.concatenate([[0], np.cumsum(EV_SPLITS)])
    seg = lambda i: w_in[:, offs[i]:offs[i + 1]]
    q_a, k_a, v_a, lr_f, lr_b, z_a, q_b, k_b, v_b, z_b = [seg(i) for i in range(10)]
    D = w_in.shape[0]

    def perm_heads(w, nh):
        return w.reshape(D, nh, ATT_HD)[:, :, _PAIR_PERM].reshape(D, nh * ATT_HD)

    lr = jnp.concatenate([lr_f, lr_b, jnp.zeros((D, LR_PAD - 2 * GLA_LOWRANK), w_in.dtype)], axis=1)
    w = jnp.concatenate(
        [q_a, k_a, v_a, z_a, perm_heads(q_b, ATT_HEADS), perm_heads(k_b, ATT_KV_HEADS), v_b, z_b, lr], axis=1)
    gw = jnp.zeros((LR_PAD, 2 * QK_A), F32)
    gw = gw.at[:GLA_LOWRANK, :QK_A].set(gate_w[0])
    gw = gw.at[GLA_LOWRANK:2 * GLA_LOWRANK, QK_A:].set(gate_w[1])
    gb = jnp.concatenate([gate_b[0], gate_b[1]])[None, :]
    return (w.astype(BF16), gw.astype(BF16), gb.astype(F32),
            q_norm_g[_PAIR_PERM][None, :], k_norm_g[_PAIR_PERM][None, :])


def _trunk(x, norm_g, final_norm_g, ev_w_in, ev_gla_gate_w, ev_gla_gate_b, ev_gla_norm_g,
           ev_q_norm_g, ev_k_norm_g, ev_w_out, od_w_in, od_pool_w, od_pool_b, od_pool_scale, od_w_out):
    B, S, D = x.shape
    cos2, sin2 = _rope_tables(S)
    tm_in = min(256, S)
    tm_out = min(512, S)
    tm_odd = min(256, S)
    gla_rows = min(512, S)
    tq = min(256, S)
    tk = min(512, S)
    for layer in range(DEPTH):
        i = layer // 2
        ng = norm_g[layer][None, :]
        if layer % 2 == 0:
            w, gw, gb, qg, kg = _even_weights(ev_w_in[i], ev_gla_gate_w[i], ev_gla_gate_b[i],
                                              ev_q_norm_g[i], ev_k_norm_g[i])
            qa, ka, va, la, sza, qb, kb, vt, szb = _ev_in(x, ng, w, gw, gb, qg, kg, cos2, sin2, tm=tm_in)
            o_bwd = _gla(qa, ka, va, la, reverse=True, rows=gla_rows)
            ma = _gla(qa, ka, va, la, reverse=False, rows=gla_rows,
                      extra=(o_bwd, sza, ev_gla_norm_g[i][None, :]))
            bound = (ATT_HD * (ATT_HD ** -0.5) * LOG2E * 1.02
                     * jnp.max(jnp.abs(ev_q_norm_g[i])) * jnp.max(jnp.abs(ev_k_norm_g[i])))
            mb = _attention(qb, kb, vt, szb, bound, tq=tq, tk=tk)
            x = _ev_out(x, ma, mb, ev_w_out[i].astype(BF16), tm=tm_out)
        else:
            fg = final_norm_g[None, :] if layer == DEPTH - 1 else None
            x = _odd(x, ng, od_w_in[i].astype(BF16), od_pool_w[i].astype(BF16), od_pool_b[i][None, :],
                     od_pool_scale[i][None, :], od_w_out[i].astype(BF16), fg, tm=tm_odd)
    return x


def kernel(x_prompt, x_sample, norm_g, final_norm_g, ev_w_in, ev_gla_gate_w, ev_gla_gate_b,
           ev_gla_norm_g, ev_q_norm_g, ev_k_norm_g, ev_w_out, od_w_in, od_pool_w, od_pool_b,
           od_pool_scale, od_w_out):
    nb = x_prompt.shape[0]
    x = jnp.concatenate([x_prompt, x_sample], axis=0)
    y = _trunk(x, norm_g, final_norm_g, ev_w_in, ev_gla_gate_w, ev_gla_gate_b, ev_gla_norm_g,
               ev_q_norm_g, ev_k_norm_g, ev_w_out, od_w_in, od_pool_w, od_pool_b, od_pool_scale, od_w_out)
    return (y[:nb], y[nb:])
```

```python
import functools

import numpy as np
import jax
import jax.numpy as jnp
from jax import lax
from jax.experimental import pallas as pl
from jax.experimental.pallas import tpu as pltpu

F32 = jnp.float32
BF16 = jnp.bfloat16

D_MODEL = 1024
DEPTH = 4
GRID_W = 64
EPS = 1e-6

GLA_HEADS = 4
GLA_DK = 128
GLA_DV = 256
GLA_LOWRANK = 16
GLA_TAU = 16.0
GLA_CHUNK = 64
GLA_SUPER = 256

ATT_HEADS = 8
ATT_KV_HEADS = 2
ATT_HD = 128
ATT_GROUP = ATT_HEADS // ATT_KV_HEADS
ROPE_THETA = 10000.0
ATT_BOUND_LIMIT = 60.0
ATT_UNROLL = 4

POOL_WINDOWS = (2, 4, 8, 16)
POOL_GROUPS = 4
POOL_WIDTH = 2 * D_MODEL
POOL_GW = POOL_WIDTH // POOL_GROUPS
POOL_HALO = 8

A_WIDTH = GLA_HEADS * GLA_DV
B_WIDTH = ATT_HEADS * ATT_HD
QK_A = GLA_HEADS * GLA_DK
KV_B = ATT_KV_HEADS * ATT_HD
EV_SPLITS = (QK_A, QK_A, A_WIDTH, GLA_LOWRANK, GLA_LOWRANK, A_WIDTH, B_WIDTH, KV_B, KV_B, B_WIDTH)

C_QA = 0
C_KA = C_QA + QK_A
C_VA = C_KA + QK_A
C_ZA = C_VA + A_WIDTH
C_QB = C_ZA + A_WIDTH
C_KB = C_QB + B_WIDTH
C_VB = C_KB + KV_B
C_ZB = C_VB + KV_B
C_LR = C_ZB + B_WIDTH
LR_PAD = 128
EV_COLS = C_LR + LR_PAD

LOG2E = 1.4426950408889634
NEG_BIG = -1e30

VMEM_LIMIT = 56 * 1024 * 1024


def _rms(x, g):
    return x * lax.rsqrt(jnp.mean(x * x, axis=-1, keepdims=True) + EPS) * g


def _silu(z):
    return z * jax.nn.sigmoid(z)


def _log_sigmoid(g):
    return jnp.minimum(g, 0.0) - jnp.log(1.0 + jnp.exp(-jnp.abs(g)))


def _const_spec(shape):
    nd = len(shape)
    return pl.BlockSpec(shape, lambda *_: (0,) * nd)


def _ev_in_kernel(x_ref, ng_ref, w_ref, gw_ref, gb_ref, qg_ref, kg_ref, cos_ref, sin_ref,
                  qa_ref, ka_ref, va_ref, la_ref, sza_ref, qb_ref, kb_ref, vt_ref, szb_ref):
    h = _rms(x_ref[...], ng_ref[...]).astype(BF16)

    def proj(c0, width):
        return jnp.dot(h, w_ref[:, c0:c0 + width], preferred_element_type=F32)

    lr = proj(C_LR, LR_PAD).astype(BF16)
    gate = jnp.dot(lr, gw_ref[...], preferred_element_type=F32) + gb_ref[...]
    la_ref[...] = _log_sigmoid(gate) * (1.0 / GLA_TAU)

    cos = cos_ref[...]
    sin = sin_ref[...]

    def norm_rope(xh, g):
        xn = _rms(xh, g)
        return xn * cos + pltpu.roll(xn, ATT_HD // 2, 1) * sin

    qscale = (ATT_HD ** -0.5) * LOG2E
    qb = proj(C_QB, B_WIDTH)
    for hh in range(ATT_HEADS):
        sl = slice(hh * ATT_HD, (hh + 1) * ATT_HD)
        qb_ref[:, sl] = (norm_rope(qb[:, sl], qg_ref[...]) * qscale).astype(BF16)
    kb = proj(C_KB, KV_B)
    for hh in range(ATT_KV_HEADS):
        sl = slice(hh * ATT_HD, (hh + 1) * ATT_HD)
        kb_ref[:, sl] = norm_rope(kb[:, sl], kg_ref[...]).astype(BF16)

    vt_ref[...] = proj(C_VB, KV_B).T.astype(BF16)

    sza_ref[...] = _silu(proj(C_ZA, A_WIDTH)).astype(BF16)
    szb_ref[...] = _silu(proj(C_ZB, B_WIDTH)).astype(BF16)
    qa_ref[...] = (proj(C_QA, QK_A) * (GLA_DK ** -0.5)).astype(BF16)
    ka_ref[...] = proj(C_KA, QK_A).astype(BF16)
    va_ref[...] = proj(C_VA, A_WIDTH).astype(BF16)


def _ev_in(x, ng, w, gw, gb, qg, kg, cos2, sin2, *, tm):
    B, S, D = x.shape
    nt = S // tm
    row = lambda width: pl.BlockSpec((None, tm, width), lambda b, i: (b, i, 0))
    outs = [
        (QK_A, BF16), (QK_A, BF16), (A_WIDTH, BF16), (2 * QK_A, F32), (A_WIDTH, BF16),
        (B_WIDTH, BF16), (KV_B, BF16), None, (B_WIDTH, BF16),
    ]
    out_shape, out_specs = [], []
    for o in outs:
        if o is None:
            out_shape.append(jax.ShapeDtypeStruct((B, KV_B, S), BF16))
            out_specs.append(pl.BlockSpec((None, KV_B, tm), lambda b, i: (b, 0, i)))
        else:
            out_shape.append(jax.ShapeDtypeStruct((B, S, o[0]), o[1]))
            out_specs.append(row(o[0]))
    return pl.pallas_call(
        _ev_in_kernel,
        grid=(B, nt),
        in_specs=[
            row(D), _const_spec((1, D)), _const_spec((D, EV_COLS)), _const_spec((LR_PAD, 2 * QK_A)),
            _const_spec((1, 2 * QK_A)), _const_spec((1, ATT_HD)), _const_spec((1, ATT_HD)),
            pl.BlockSpec((tm, ATT_HD), lambda b, i: (i, 0)), pl.BlockSpec((tm, ATT_HD), lambda b, i: (i, 0)),
        ],
        out_specs=out_specs,
        out_shape=out_shape,
        compiler_params=pltpu.CompilerParams(
            dimension_semantics=("parallel", "parallel"), vmem_limit_bytes=VMEM_LIMIT),
        name="ev_in",
    )(x, ng, w, gw, gb, qg, kg, cos2, sin2)


def _split_hi_lo(a):
    hi = a.astype(BF16)
    lo = (a - hi.astype(F32)).astype(BF16)
    return hi, lo


def _gla_kernel(*refs, reverse, final, n_super):
    if final:
        q_ref, k_ref, v_ref, la_ref, ob_ref, sz_ref, g_ref, o_ref, st_ref = refs
    else:
        q_ref, k_ref, v_ref, la_ref, o_ref, st_ref = refs
    SC, C = GLA_SUPER, GLA_CHUNK
    n_chunk = SC // C

    @pl.when(pl.program_id(1) == 0)
    def _():
        st_ref[...] = jnp.zeros_like(st_ref)

    ri = lax.broadcasted_iota(jnp.int32, (SC, SC), 0)
    ci = lax.broadcasted_iota(jnp.int32, (SC, SC), 1)
    same = (ri // C) == (ci // C)
    if reverse:
        incl = same & (ci >= ri)
        incl_t = same & (ri >= ci)
        strict_t = same & (ri < ci)
    else:
        incl = same & (ci <= ri)
        incl_t = same & (ri <= ci)
        strict_t = same & (ri > ci)
    incl_bf = incl.astype(BF16)
    incl_t_bf = incl_t.astype(BF16)
    strict_t_bf = strict_t.astype(BF16)
    lane_chunk = lax.broadcasted_iota(jnp.int32, (GLA_DK, SC), 1) // C

    order = range(n_super - 1, -1, -1) if reverse else range(n_super)
    corder = range(n_chunk - 1, -1, -1) if reverse else range(n_chunk)
    bodies = [(sc, hd) for sc in order for hd in range(GLA_HEADS)]
    rows = {sc: slice(sc * SC, (sc + 1) * SC) for sc in order}
    kcol = [slice(hd * GLA_DK, (hd + 1) * GLA_DK) for hd in range(GLA_HEADS)]
    vcol = [slice(hd * GLA_DV, (hd + 1) * GLA_DV) for hd in range(GLA_HEADS)]
    dot = functools.partial(jnp.dot, preferred_element_type=F32)

    hl, hl_t, k_t = {}, {}, {}
    for key in bodies:
        sc, hd = key
        la = la_ref[rows[sc], kcol[hd]]
        hi, lo = _split_hi_lo(la)
        hl[key] = jnp.concatenate([hi, lo], axis=1)
        hi_t, lo_t = _split_hi_lo(la.T)
        hl_t[key] = jnp.concatenate([hi_t, lo_t], axis=0)
        k_t[key] = k_ref[rows[sc], kcol[hd]].astype(F32).T
    b2 = {key: dot(incl_bf, hl[key]) for key in bodies}
    bt2 = {key: dot(hl_t[key], incl_t_bf) for key in bodies}
    gt2 = {key: dot(hl_t[key], strict_t_bf) for key in bodies}

    q_dec, k_inv_t, k_end_t, tot_t = {}, {}, {}, {}
    for key in bodies:
        sc, hd = key
        b = b2[key][:, :GLA_DK] + b2[key][:, GLA_DK:]
        b_t = bt2[key][:GLA_DK] + bt2[key][GLA_DK:]
        g_t = gt2[key][:GLA_DK] + gt2[key][GLA_DK:]
        tot_t[key] = b_t + g_t
        q_dec[key] = (q_ref[rows[sc], kcol[hd]].astype(F32) * jnp.exp(b)).astype(BF16)
        k_inv_t[key] = (k_t[key] * jnp.exp(-b_t)).astype(BF16)
        k_end_t[key] = k_t[key] * jnp.exp(g_t)

    scores = {key: dot(q_dec[key], k_inv_t[key]) for key in bodies}
    o_sc = {}
    for key in bodies:
        sc, hd = key
        o_sc[key] = dot(jnp.where(incl, scores[key], 0.0).astype(BF16), v_ref[rows[sc], vcol[hd]])
    kv = {}
    for key in bodies:
        sc, hd = key
        for c in corder:
            k_end_c = jnp.where(lane_chunk == c, k_end_t[key], 0.0).astype(BF16)
            kv[key, c] = dot(k_end_c, v_ref[rows[sc], vcol[hd]])

    inter = {}
    for sc in order:
        for c in corder:
            for hd in range(GLA_HEADS):
                key = (sc, hd)
                state = st_ref[hd]
                inter[key, c] = dot(q_dec[key][c * C:(c + 1) * C, :], state.astype(BF16))
                st_ref[hd] = state * jnp.exp(tot_t[key][:, c * C:c * C + 1]) + kv[key, c]

    for key in bodies:
        sc, hd = key
        o = o_sc[key] + jnp.concatenate([inter[key, c] for c in range(n_chunk)], axis=0)
        if final:
            o = o + ob_ref[rows[sc], vcol[hd]]
            o = _rms(o, g_ref[:, vcol[hd]]) * sz_ref[rows[sc], vcol[hd]].astype(F32)
        o_ref[rows[sc], vcol[hd]] = o.astype(o_ref.dtype)


def _gla(q, k, v, la, *, reverse, rows, extra=None):
    B, S, _ = q.shape
    nb = S // rows
    final = extra is not None
    blk = (lambda i: nb - 1 - i) if reverse else (lambda i: i)
    la_blk = 1 if reverse else 0
    qk_spec = pl.BlockSpec((None, rows, QK_A), lambda b, i: (b, blk(i), 0))
    v_spec = pl.BlockSpec((None, rows, A_WIDTH), lambda b, i: (b, blk(i), 0))
    la_spec = pl.BlockSpec((None, rows, QK_A), lambda b, i: (b, blk(i), la_blk))
    in_specs = [qk_spec, qk_spec, v_spec, la_spec]
    args = [q, k, v, la]
    if final:
        ob, sz, g = extra
        in_specs += [v_spec, v_spec, _const_spec((1, A_WIDTH))]
        args += [ob, sz, g]
    return pl.pallas_call(
        functools.partial(_gla_kernel, reverse=reverse, final=final, n_super=rows // GLA_SUPER),
        grid=(B, nb),
        in_specs=in_specs,
        out_specs=v_spec,
        out_shape=jax.ShapeDtypeStruct((B, S, A_WIDTH), BF16 if final else F32),
        scratch_shapes=[pltpu.VMEM((GLA_HEADS, GLA_DK, GLA_DV), F32)],
        compiler_params=pltpu.CompilerParams(
            dimension_semantics=("parallel", "arbitrary"), vmem_limit_bytes=VMEM_LIMIT),
        name="gla_fwd" if final else "gla_bwd",
    )(*args)


def _load_qt(q_ref, qt_ref, tq):
    for hh in range(ATT_GROUP):
        qt_ref[:, hh * tq:(hh + 1) * tq] = q_ref[:, hh * ATT_HD:(hh + 1) * ATT_HD].astype(F32).T.astype(BF16)


def _attn_finish(acc_ref, l_ref, sz_ref, o_ref, tq):
    o_t = acc_ref[...] * (1.0 / jnp.sum(l_ref[...], axis=0, keepdims=True))
    for hh in range(ATT_GROUP):
        sl = slice(hh * ATT_HD, (hh + 1) * ATT_HD)
        o_ref[:, sl] = (o_t[:, hh * tq:(hh + 1) * tq].T * sz_ref[:, sl].astype(F32)).astype(o_ref.dtype)


def _sublane_partial_sums(p):
    return p.reshape(p.shape[0] // 8, 8, p.shape[1]).sum(axis=0)


def _attn_bounded_kernel(q_ref, k_ref, vt_ref, sz_ref, o_ref, qt_ref, p_ref, acc_ref, l_ref, *, tk):
    S = k_ref.shape[0]
    tq = q_ref.shape[0]
    nk = S // tk
    _load_qt(q_ref, qt_ref, tq)
    acc_ref[...] = jnp.zeros_like(acc_ref)
    l_ref[...] = jnp.zeros_like(l_ref)
    p_ref[1] = jnp.zeros_like(p_ref[1])

    def probs(j, slot):
        off = pl.multiple_of(j * tk, tk)
        s = jnp.dot(k_ref[pl.ds(off, tk), :], qt_ref[...], preferred_element_type=F32)
        p = jnp.exp2(s)
        l_ref[...] += _sublane_partial_sums(p)
        p_ref[slot] = p.astype(BF16)

    def accumulate(j, slot):
        off = pl.multiple_of(j * tk, tk)
        acc_ref[...] += jnp.dot(vt_ref[:, pl.ds(off, tk)], p_ref[slot], preferred_element_type=F32)

    unroll = min(ATT_UNROLL, nk)

    def group(jj, carry):
        j = unroll * jj
        for u in range(unroll):
            probs(j + u, u % 2)
            accumulate(jnp.maximum(j + u - 1, 0), (u + 1) % 2)
        return carry

    lax.fori_loop(0, nk // unroll, group, 0)
    accumulate(nk - 1, (nk - 1) % 2)
    _attn_finish(acc_ref, l_ref, sz_ref, o_ref, tq)


def _attn_online_kernel(q_ref, k_ref, vt_ref, sz_ref, o_ref, qt_ref, s_ref, mx_ref, acc_ref, l_ref, m_ref, *, tk):
    S = k_ref.shape[0]
    tq = q_ref.shape[0]
    nk = S // tk
    _load_qt(q_ref, qt_ref, tq)
    acc_ref[...] = jnp.zeros_like(acc_ref)
    l_ref[...] = jnp.zeros_like(l_ref)
    m_ref[...] = jnp.full_like(m_ref, NEG_BIG)

    def scores(j, slot):
        off = pl.multiple_of(j * tk, tk)
        s = jnp.dot(k_ref[pl.ds(off, tk), :], qt_ref[...], preferred_element_type=F32)
        s_ref[slot] = s
        mx_ref[slot] = jnp.max(s, axis=0, keepdims=True)

    def accumulate(j, slot):
        off = pl.multiple_of(j * tk, tk)
        m_old = m_ref[...]
        m_new = jnp.maximum(m_old, mx_ref[slot])
        alpha = jnp.exp2(m_old - m_new)
        p = jnp.exp2(s_ref[slot] - m_new)
        l_ref[...] = l_ref[...] * alpha + _sublane_partial_sums(p)
        pv = jnp.dot(vt_ref[:, pl.ds(off, tk)], p.astype(BF16), preferred_element_type=F32)
        acc_ref[...] = acc_ref[...] * alpha + pv
        m_ref[...] = m_new

    scores(0, 0)

    def pair(jj, carry):
        j = 2 * jj
        scores(j + 1, 1)
        accumulate(j, 0)
        scores(jnp.minimum(j + 2, nk - 1), 0)
        accumulate(j + 1, 1)
        return carry

    lax.fori_loop(0, nk // 2, pair, 0)
    _attn_finish(acc_ref, l_ref, sz_ref, o_ref, tq)


def _attention(qb, kb, vt, szb, score_bound, *, tq, tk):
    B, S, _ = qb.shape
    nk = S // tk
    assert S % tq == 0 and nk % 2 == 0 and nk % min(ATT_UNROLL, nk) == 0
    gw = ATT_GROUP * ATT_HD
    n = ATT_GROUP * tq

    def call(body, scratch, name):
        q_spec = pl.BlockSpec((None, tq, gw), lambda b, kh, i: (b, i, kh))
        return pl.pallas_call(
            functools.partial(body, tk=tk),
            grid=(B, ATT_KV_HEADS, S // tq),
            in_specs=[
                q_spec,
                pl.BlockSpec((None, S, ATT_HD), lambda b, kh, i: (b, 0, kh)),
                pl.BlockSpec((None, ATT_HD, S), lambda b, kh, i: (b, kh, 0)),
                q_spec,
            ],
            out_specs=q_spec,
            out_shape=jax.ShapeDtypeStruct((B, S, B_WIDTH), BF16),
            scratch_shapes=scratch,
            compiler_params=pltpu.CompilerParams(
                dimension_semantics=("parallel", "parallel", "arbitrary"), vmem_limit_bytes=VMEM_LIMIT),
            name=name)

    bounded = call(_attn_bounded_kernel, [
        pltpu.VMEM((ATT_HD, n), BF16), pltpu.VMEM((2, tk, n), BF16),
        pltpu.VMEM((ATT_HD, n), F32), pltpu.VMEM((8, n), F32)], "attn_bounded")
    online = call(_attn_online_kernel, [
        pltpu.VMEM((ATT_HD, n), BF16), pltpu.VMEM((2, tk, n), F32), pltpu.VMEM((2, 1, n), F32),
        pltpu.VMEM((ATT_HD, n), F32), pltpu.VMEM((8, n), F32), pltpu.VMEM((1, n), F32)], "attn_online")
    return lax.cond(score_bound < ATT_BOUND_LIMIT, bounded, online, qb, kb, vt, szb)


def _ev_out_kernel(x_ref, ma_ref, mb_ref, w_ref, o_ref):
    acc = jnp.dot(ma_ref[...], w_ref[:A_WIDTH, :], preferred_element_type=F32)
    acc = acc + jnp.dot(mb_ref[...], w_ref[A_WIDTH:, :], preferred_element_type=F32)
    o_ref[...] = x_ref[...] + acc


def _ev_out(x, ma, mb, w, *, tm):
    B, S, D = x.shape
    row = lambda width: pl.BlockSpec((None, tm, width), lambda b, i: (b, i, 0))
    return pl.pallas_call(
        _ev_out_kernel,
        grid=(B, S // tm),
        in_specs=[row(D), row(A_WIDTH), row(B_WIDTH), _const_spec((A_WIDTH + B_WIDTH, D))],
        out_specs=row(D),
        out_shape=jax.ShapeDtypeStruct((B, S, D), F32),
        compiler_params=pltpu.CompilerParams(
            dimension_semantics=("parallel", "parallel"), vmem_limit_bytes=VMEM_LIMIT),
        name="ev_out",
    )(x, ma, mb, w)


def _odd_kernel(*refs, seq_len, final):
    if final:
        xp_ref, x_ref, xn_ref, ng_ref, win_ref, pw_ref, pb_ref, ps_ref, wout_ref, fg_ref, o_ref = refs
    else:
        xp_ref, x_ref, xn_ref, ng_ref, win_ref, pw_ref, pb_ref, ps_ref, wout_ref, o_ref = refs
    tm = x_ref.shape[0]
    H = POOL_HALO
    t0 = pl.program_id(1) * tm
    x = x_ref[...]
    xe = jnp.concatenate([x, xp_ref[...], xn_ref[...]], axis=0)
    h = _rms(xe, ng_ref[...]).astype(BF16)
    ue = jnp.dot(h, win_ref[:, :POOL_WIDTH], preferred_element_type=F32)
    z = jnp.dot(h[:tm], win_ref[:, POOL_WIDTH:], preferred_element_type=F32)
    ue_bf = ue.astype(BF16)

    t = t0 + lax.broadcasted_iota(jnp.int32, (tm, 1), 0)
    c = lax.broadcasted_iota(jnp.int32, (1, tm + 2 * H), 1)
    pos = t0 + jnp.where(c < tm, c, jnp.where(c < tm + H, c - tm - H, c - H))
    valid = (pos >= 0) & (pos < seq_len)

    pooled, inv_cnt = [], []
    for g, w in enumerate(POOL_WINDOWS):
        lo = t - w // 2
        hi = lo + w
        band = ((pos >= lo) & (pos < hi) & valid).astype(BF16)
        inv_cnt.append(1.0 / (jnp.minimum(hi, seq_len) - jnp.maximum(lo, 0)).astype(F32))
        pooled.append(jnp.dot(band, ue_bf[:, g * POOL_GW:(g + 1) * POOL_GW], preferred_element_type=F32))
    ms = []
    for g in range(POOL_GROUPS):
        d = (pooled[g] * inv_cnt[g] - ue[:tm, g * POOL_GW:(g + 1) * POOL_GW]).astype(BF16)
        ms.append(jnp.dot(d, pw_ref[g], preferred_element_type=F32))
    m = jnp.concatenate(ms, axis=1) + pb_ref[...]
    m = (m * ps_ref[...] * _silu(z)).astype(BF16)
    y = x + jnp.dot(m, wout_ref[...], preferred_element_type=F32)
    if final:
        y = _rms(y, fg_ref[...])
    o_ref[...] = y


def _odd(x, ng, win, pw, pb, ps, wout, fg, *, tm):
    B, S, D = x.shape
    H = POOL_HALO
    nh = tm // H
    last = S // H - 1
    row = pl.BlockSpec((None, tm, D), lambda b, i: (b, i, 0))
    prev = pl.BlockSpec((None, H, D), lambda b, i: (b, jnp.maximum(i * nh - 1, 0), 0))
    nxt = pl.BlockSpec((None, H, D), lambda b, i: (b, jnp.minimum((i + 1) * nh, last), 0))
    final = fg is not None
    in_specs = [
        prev, row, nxt, _const_spec((1, D)), _const_spec((D, 2 * POOL_WIDTH)),
        _const_spec((POOL_GROUPS, POOL_GW, POOL_GW)), _const_spec((1, POOL_WIDTH)),
        _const_spec((1, POOL_WIDTH)), _const_spec((POOL_WIDTH, D)),
    ]
    args = [x, x, x, ng, win, pw, pb, ps, wout]
    if final:
        in_specs.append(_const_spec((1, D)))
        args.append(fg)
    return pl.pallas_call(
        functools.partial(_odd_kernel, seq_len=S, final=final),
        grid=(B, S // tm),
        in_specs=in_specs,
        out_specs=row,
        out_shape=jax.ShapeDtypeStruct((B, S, D), F32),
        compiler_params=pltpu.CompilerParams(
            dimension_semantics=("parallel", "parallel"), vmem_limit_bytes=VMEM_LIMIT),
        name="odd_final" if final else "odd",
    )(*args)


def _rope_tables(S):
    rows = S // GRID_W
    row = jnp.repeat(jnp.arange(rows, dtype=F32), GRID_W)
    col = jnp.tile(jnp.arange(GRID_W, dtype=F32), rows)
    n_pairs = ATT_HD // 4
    freqs = ROPE_THETA ** (-jnp.arange(n_pairs, dtype=F32) / n_pairs)
    ang = jnp.concatenate([row[:, None] * freqs, col[:, None] * freqs], axis=-1)
    c, s = jnp.cos(ang), jnp.sin(ang)
    return jnp.concatenate([c, c], axis=-1), jnp.concatenate([-s, s], axis=-1)


_PAIR_PERM = np.concatenate([np.arange(0, ATT_HD, 2), np.arange(1, ATT_HD, 2)])


def _even_weights(w_in, gate_w, gate_b, q_norm_g, k_norm_g):
    offs = np.concatenate([[0], np.cumsum(EV_SPLITS)])
    seg = lambda i: w_in[:, offs[i]:offs[i + 1]]
    q_a, k_a, v_a, lr_f, lr_b, z_a, q_b, k_b, v_b, z_b = [seg(i) for i in range(10)]
    D = w_in.shape[0]

    def perm_heads(w, nh):
        return w.reshape(D, nh, ATT_HD)[:, :, _PAIR_PERM].reshape(D, nh * ATT_HD)

    lr = jnp.concatenate([lr_f, lr_b, jnp.zeros((D, LR_PAD - 2 * GLA_LOWRANK), w_in.dtype)], axis=1)
    w = jnp.concatenate(
        [q_a, k_a, v_a, z_a, perm_heads(q_b, ATT_HEADS), perm_heads(k_b, ATT_KV_HEADS), v_b, z_b, lr], axis=1)
    gw = jnp.zeros((LR_PAD, 2 * QK_A), F32)
    gw = gw.at[:GLA_LOWRANK, :QK_A].set(gate_w[0])
    gw = gw.at[GLA_LOWRANK:2 * GLA_LOWRANK, QK_A:].set(gate_w[1])
    gb = jnp.concatenate([gate_b[0], gate_b[1]])[None, :]
    return (w.astype(BF16), gw.astype(BF16), gb.astype(F32),
            q_norm_g[_PAIR_PERM][None, :], k_norm_g[_PAIR_PERM][None, :])


def _prepare(norm_g, final_norm_g, ev_w_in, ev_gla_gate_w, ev_gla_gate_b, ev_gla_norm_g,
             ev_q_norm_g, ev_k_norm_g, ev_w_out, od_w_in, od_pool_w, od_pool_b, od_pool_scale, od_w_out):
    layers = []
    for layer in range(DEPTH):
        i = layer // 2
        ng = norm_g[layer][None, :]
        if layer % 2 == 0:
            bound = (ATT_HD * (ATT_HD ** -0.5) * LOG2E * 1.02
                     * jnp.max(jnp.abs(ev_q_norm_g[i])) * jnp.max(jnp.abs(ev_k_norm_g[i])))
            layers.append((ng, _even_weights(ev_w_in[i], ev_gla_gate_w[i], ev_gla_gate_b[i],
                                             ev_q_norm_g[i], ev_k_norm_g[i]),
                           ev_gla_norm_g[i][None, :], bound, ev_w_out[i].astype(BF16)))
        else:
            fg = final_norm_g[None, :] if layer == DEPTH - 1 else None
            layers.append((ng, od_w_in[i].astype(BF16), od_pool_w[i].astype(BF16), od_pool_b[i][None, :],
                           od_pool_scale[i][None, :], od_w_out[i].astype(BF16), fg))
    return layers


def _run(x, layers, rope):
    B, S, D = x.shape
    cos2, sin2 = rope
    tm_in = min(256, S)
    tm_out = min(512, S)
    tm_odd = min(256, S)
    gla_rows = min(512, S)
    tq = min(512, S)
    tk = min(512, S)
    for layer, ops in enumerate(layers):
        if layer % 2 == 0:
            ng, (w, gw, gb, qg, kg), gla_g, bound, w_out = ops
            qa, ka, va, la, sza, qb, kb, vt, szb = _ev_in(x, ng, w, gw, gb, qg, kg, cos2, sin2, tm=tm_in)
            o_bwd = _gla(qa, ka, va, la, reverse=True, rows=gla_rows)
            ma = _gla(qa, ka, va, la, reverse=False, rows=gla_rows, extra=(o_bwd, sza, gla_g))
            mb = _attention(qb, kb, vt, szb, bound, tq=tq, tk=tk)
            x = _ev_out(x, ma, mb, w_out, tm=tm_out)
        else:
            ng, win, pw, pb, ps, wout, fg = ops
            x = _odd(x, ng, win, pw, pb, ps, wout, fg, tm=tm_odd)
    return x


def _trunk(x, *weights):
    return _run(x, _prepare(*weights), _rope_tables(x.shape[1]))


def kernel(x_prompt, x_sample, norm_g, final_norm_g, ev_w_in, ev_gla_gate_w, ev_gla_gate_b,
           ev_gla_norm_g, ev_q_norm_g, ev_k_norm_g, ev_w_out, od_w_in, od_pool_w, od_pool_b,
           od_pool_scale, od_w_out):
    layers = _prepare(norm_g, final_norm_g, ev_w_in, ev_gla_gate_w, ev_gla_gate_b, ev_gla_norm_g,
                      ev_q_norm_g, ev_k_norm_g, ev_w_out, od_w_in, od_pool_w, od_pool_b, od_pool_scale,
                      od_w_out)
    rope = _rope_tables(x_prompt.shape[1])
    rope_s = rope if x_sample.shape[1] == x_prompt.shape[1] else _rope_tables(x_sample.shape[1])
    return (_run(x_prompt, layers, rope), _run(x_sample, layers, rope_s))
```

```python
import functools

import numpy as np
import jax
import jax.numpy as jnp
from jax import lax
from jax.experimental import pallas as pl
from jax.experimental.pallas import tpu as pltpu

F32 = jnp.float32
BF16 = jnp.bfloat16

D_MODEL = 1024
DEPTH = 4
GRID_W = 64
EPS = 1e-6

GLA_HEADS = 4
GLA_DK = 128
GLA_DV = 256
GLA_LOWRANK = 16
GLA_TAU = 16.0
GLA_CHUNK = 64
GLA_SUPER = 256

ATT_HEADS = 8
ATT_KV_HEADS = 2
ATT_HD = 128
ATT_GROUP = ATT_HEADS // ATT_KV_HEADS
ROPE_THETA = 10000.0
ATT_BOUND_LIMIT = 60.0
ATT_UNROLL = 2

POOL_WINDOWS = (2, 4, 8, 16)
POOL_GROUPS = 4
POOL_WIDTH = 2 * D_MODEL
POOL_GW = POOL_WIDTH // POOL_GROUPS
POOL_HALO = 8

A_WIDTH = GLA_HEADS * GLA_DV
B_WIDTH = ATT_HEADS * ATT_HD
QK_A = GLA_HEADS * GLA_DK
KV_B = ATT_KV_HEADS * ATT_HD
EV_SPLITS = (QK_A, QK_A, A_WIDTH, GLA_LOWRANK, GLA_LOWRANK, A_WIDTH, B_WIDTH, KV_B, KV_B, B_WIDTH)

C_QA = 0
C_KA = C_QA + QK_A
C_VA = C_KA + QK_A
C_ZA = C_VA + A_WIDTH
C_QB = C_ZA + A_WIDTH
C_KB = C_QB + B_WIDTH
C_VB = C_KB + KV_B
C_ZB = C_VB + KV_B
C_LR = C_ZB + B_WIDTH
LR_PAD = 128
EV_COLS = C_LR + LR_PAD

LOG2E = 1.4426950408889634
NEG_BIG = -1e30

VMEM_LIMIT = 56 * 1024 * 1024


def _rms(x, g):
    return x * lax.rsqrt(jnp.mean(x * x, axis=-1, keepdims=True) + EPS) * g


def _silu(z):
    return z * jax.nn.sigmoid(z)


def _log_sigmoid(g):
    return jnp.minimum(g, 0.0) - jnp.log(1.0 + jnp.exp(-jnp.abs(g)))


def _const_spec(shape):
    nd = len(shape)
    return pl.BlockSpec(shape, lambda *_: (0,) * nd, pipeline_mode=pl.Buffered(1))


def _ev_in_kernel(x_ref, ng_ref, w_ref, gw_ref, gb_ref, qg_ref, kg_ref, cos_ref, sin_ref,
                  qa_ref, ka_ref, va_ref, la_ref, sza_ref, qb_ref, kb_ref, vt_ref, szb_ref):
    h = _rms(x_ref[...], ng_ref[...]).astype(BF16)

    def proj(c0, width):
        return jnp.dot(h, w_ref[:, c0:c0 + width], preferred_element_type=F32)

    lr = proj(C_LR, LR_PAD).astype(BF16)
    gate = jnp.dot(lr, gw_ref[...], preferred_element_type=F32) + gb_ref[...]
    la_ref[...] = _log_sigmoid(gate) * (1.0 / GLA_TAU)

    cos = cos_ref[...]
    sin = sin_ref[...]

    def norm_rope(xh, g):
        xn = _rms(xh, g)
        return xn * cos + pltpu.roll(xn, ATT_HD // 2, 1) * sin

    qscale = (ATT_HD ** -0.5) * LOG2E
    qb = proj(C_QB, B_WIDTH)
    for hh in range(ATT_HEADS):
        sl = slice(hh * ATT_HD, (hh + 1) * ATT_HD)
        qb_ref[:, sl] = (norm_rope(qb[:, sl], qg_ref[...]) * qscale).astype(BF16)
    kb = proj(C_KB, KV_B)
    for hh in range(ATT_KV_HEADS):
        sl = slice(hh * ATT_HD, (hh + 1) * ATT_HD)
        kb_ref[:, sl] = norm_rope(kb[:, sl], kg_ref[...]).astype(BF16)

    vt_ref[...] = proj(C_VB, KV_B).T.astype(BF16)

    sza_ref[...] = _silu(proj(C_ZA, A_WIDTH)).astype(BF16)
    szb_ref[...] = _silu(proj(C_ZB, B_WIDTH)).astype(BF16)
    qa_ref[...] = (proj(C_QA, QK_A) * (GLA_DK ** -0.5)).astype(BF16)
    ka_ref[...] = proj(C_KA, QK_A).astype(BF16)
    va_ref[...] = proj(C_VA, A_WIDTH).astype(BF16)


def _ev_in(x, ng, w, gw, gb, qg, kg, cos2, sin2, *, tm):
    B, S, D = x.shape
    nt = S // tm
    row = lambda width: pl.BlockSpec((None, tm, width), lambda b, i: (b, i, 0))
    outs = [
        (QK_A, BF16), (QK_A, BF16), (A_WIDTH, BF16), (2 * QK_A, F32), (A_WIDTH, BF16),
        (B_WIDTH, BF16), (KV_B, BF16), None, (B_WIDTH, BF16),
    ]
    out_shape, out_specs = [], []
    for o in outs:
        if o is None:
            out_shape.append(jax.ShapeDtypeStruct((B, KV_B, S), BF16))
            out_specs.append(pl.BlockSpec((None, KV_B, tm), lambda b, i: (b, 0, i)))
        else:
            out_shape.append(jax.ShapeDtypeStruct((B, S, o[0]), o[1]))
            out_specs.append(row(o[0]))
    return pl.pallas_call(
        _ev_in_kernel,
        grid=(B, nt),
        in_specs=[
            row(D), _const_spec((1, D)), _const_spec((D, EV_COLS)), _const_spec((LR_PAD, 2 * QK_A)),
            _const_spec((1, 2 * QK_A)), _const_spec((1, ATT_HD)), _const_spec((1, ATT_HD)),
            pl.BlockSpec((tm, ATT_HD), lambda b, i: (i, 0)), pl.BlockSpec((tm, ATT_HD), lambda b, i: (i, 0)),
        ],
        out_specs=out_specs,
        out_shape=out_shape,
        compiler_params=pltpu.CompilerParams(
            dimension_semantics=("parallel", "parallel"), vmem_limit_bytes=VMEM_LIMIT),
        name="ev_in",
    )(x, ng, w, gw, gb, qg, kg, cos2, sin2)


def _split_hi_lo(a):
    hi = a.astype(BF16)
    lo = (a - hi.astype(F32)).astype(BF16)
    return hi, lo


def _gla_kernel(*refs, reverse, final, n_super):
    if final:
        q_ref, k_ref, v_ref, la_ref, ob_ref, sz_ref, g_ref, o_ref, st_ref = refs
    else:
        q_ref, k_ref, v_ref, la_ref, o_ref, st_ref = refs
    SC, C = GLA_SUPER, GLA_CHUNK
    n_chunk = SC // C

    @pl.when(pl.program_id(1) == 0)
    def _():
        st_ref[...] = jnp.zeros_like(st_ref)

    ri = lax.broadcasted_iota(jnp.int32, (SC, SC), 0)
    ci = lax.broadcasted_iota(jnp.int32, (SC, SC), 1)
    same = (ri // C) == (ci // C)
    if reverse:
        incl = same & (ci >= ri)
        incl_t = same & (ri >= ci)
        strict_t = same & (ri < ci)
    else:
        incl = same & (ci <= ri)
        incl_t = same & (ri <= ci)
        strict_t = same & (ri > ci)
    incl_bf = incl.astype(BF16)
    incl_t_bf = incl_t.astype(BF16)
    strict_t_bf = strict_t.astype(BF16)
    lane_chunk = lax.broadcasted_iota(jnp.int32, (GLA_DK, SC), 1) // C

    order = range(n_super - 1, -1, -1) if reverse else range(n_super)
    corder = range(n_chunk - 1, -1, -1) if reverse else range(n_chunk)
    bodies = [(sc, hd) for sc in order for hd in range(GLA_HEADS)]
    rows = {sc: slice(sc * SC, (sc + 1) * SC) for sc in order}
    kcol = [slice(hd * GLA_DK, (hd + 1) * GLA_DK) for hd in range(GLA_HEADS)]
    vcol = [slice(hd * GLA_DV, (hd + 1) * GLA_DV) for hd in range(GLA_HEADS)]
    dot = functools.partial(jnp.dot, preferred_element_type=F32)

    hl, hl_t, k_t = {}, {}, {}
    for key in bodies:
        sc, hd = key
        la = la_ref[rows[sc], kcol[hd]]
        hi, lo = _split_hi_lo(la)
        hl[key] = jnp.concatenate([hi, lo], axis=1)
        hi_t, lo_t = _split_hi_lo(la.T)
        hl_t[key] = jnp.concatenate([hi_t, lo_t], axis=0)
        k_t[key] = k_ref[rows[sc], kcol[hd]].astype(F32).T
    b2 = {key: dot(incl_bf, hl[key]) for key in bodies}
    bt2 = {key: dot(hl_t[key], incl_t_bf) for key in bodies}
    gt2 = {key: dot(hl_t[key], strict_t_bf) for key in bodies}

    q_dec, k_inv_t, k_end_t, tot_t = {}, {}, {}, {}
    for key in bodies:
        sc, hd = key
        b = b2[key][:, :GLA_DK] + b2[key][:, GLA_DK:]
        b_t = bt2[key][:GLA_DK] + bt2[key][GLA_DK:]
        g_t = gt2[key][:GLA_DK] + gt2[key][GLA_DK:]
        tot_t[key] = b_t + g_t
        q_dec[key] = (q_ref[rows[sc], kcol[hd]].astype(F32) * jnp.exp(b)).astype(BF16)
        k_inv_t[key] = (k_t[key] * jnp.exp(-b_t)).astype(BF16)
        k_end_t[key] = k_t[key] * jnp.exp(g_t)

    scores = {key: dot(q_dec[key], k_inv_t[key]) for key in bodies}
    o_sc = {}
    for key in bodies:
        sc, hd = key
        o_sc[key] = dot(jnp.where(incl, scores[key], 0.0).astype(BF16), v_ref[rows[sc], vcol[hd]])
    kv = {}
    for key in bodies:
        sc, hd = key
        for c in corder:
            k_end_c = jnp.where(lane_chunk == c, k_end_t[key], 0.0).astype(BF16)
            kv[key, c] = dot(k_end_c, v_ref[rows[sc], vcol[hd]])

    inter = {}
    for sc in order:
        for c in corder:
            for hd in range(GLA_HEADS):
                key = (sc, hd)
                state = st_ref[hd]
                inter[key, c] = dot(q_dec[key][c * C:(c + 1) * C, :], state.astype(BF16))
                st_ref[hd] = state * jnp.exp(tot_t[key][:, c * C:c * C + 1]) + kv[key, c]

    for key in bodies:
        sc, hd = key
        o = o_sc[key] + jnp.concatenate([inter[key, c] for c in range(n_chunk)], axis=0)
        if final:
            o = o + ob_ref[rows[sc], vcol[hd]]
            o = _rms(o, g_ref[:, vcol[hd]]) * sz_ref[rows[sc], vcol[hd]].astype(F32)
        o_ref[rows[sc], vcol[hd]] = o.astype(o_ref.dtype)


def _gla(q, k, v, la, *, reverse, rows, extra=None):
    B, S, _ = q.shape
    nb = S // rows
    final = extra is not None
    blk = (lambda i: nb - 1 - i) if reverse else (lambda i: i)
    la_blk = 1 if reverse else 0
    qk_spec = pl.BlockSpec((None, rows, QK_A), lambda b, i: (b, blk(i), 0))
    v_spec = pl.BlockSpec((None, rows, A_WIDTH), lambda b, i: (b, blk(i), 0))
    la_spec = pl.BlockSpec((None, rows, QK_A), lambda b, i: (b, blk(i), la_blk))
    in_specs = [qk_spec, qk_spec, v_spec, la_spec]
    args = [q, k, v, la]
    if final:
        ob, sz, g = extra
        in_specs += [v_spec, v_spec, _const_spec((1, A_WIDTH))]
        args += [ob, sz, g]
    return pl.pallas_call(
        functools.partial(_gla_kernel, reverse=reverse, final=final, n_super=rows // GLA_SUPER),
        grid=(B, nb),
        in_specs=in_specs,
        out_specs=v_spec,
        out_shape=jax.ShapeDtypeStruct((B, S, A_WIDTH), BF16 if final else F32),
        scratch_shapes=[pltpu.VMEM((GLA_HEADS, GLA_DK, GLA_DV), F32)],
        compiler_params=pltpu.CompilerParams(
            dimension_semantics=("parallel", "arbitrary"), vmem_limit_bytes=VMEM_LIMIT),
        name="gla_fwd" if final else "gla_bwd",
    )(*args)


def _load_qt(q_ref, qt_ref, tq):
    for hh in range(ATT_GROUP):
        qt_ref[:, hh * tq:(hh + 1) * tq] = q_ref[:, hh * ATT_HD:(hh + 1) * ATT_HD].astype(F32).T.astype(BF16)


def _attn_finish(acc_ref, l_ref, sz_ref, o_ref, tq):
    o_t = acc_ref[...] * (1.0 / jnp.sum(l_ref[...], axis=0, keepdims=True))
    for hh in range(ATT_GROUP):
        sl = slice(hh * ATT_HD, (hh + 1) * ATT_HD)
        o_ref[:, sl] = (o_t[:, hh * tq:(hh + 1) * tq].T * sz_ref[:, sl].astype(F32)).astype(o_ref.dtype)


def _sublane_partial_sums(p):
    return p.reshape(p.shape[0] // 8, 8, p.shape[1]).sum(axis=0)


def _attn_bounded_kernel(q_ref, k_ref, vt_ref, sz_ref, o_ref, qt_ref, p_ref, acc_ref, l_ref, *, tk):
    S = k_ref.shape[0]
    tq = q_ref.shape[0]
    nk = S // tk
    _load_qt(q_ref, qt_ref, tq)
    acc_ref[...] = jnp.zeros_like(acc_ref)
    l_ref[...] = jnp.zeros_like(l_ref)
    p_ref[1] = jnp.zeros_like(p_ref[1])

    def probs(j, slot):
        off = pl.multiple_of(j * tk, tk)
        s = jnp.dot(k_ref[pl.ds(off, tk), :], qt_ref[...], preferred_element_type=F32)
        p = jnp.exp2(s)
        l_ref[...] += _sublane_partial_sums(p)
        p_ref[slot] = p.astype(BF16)

    def accumulate(j, slot):
        off = pl.multiple_of(j * tk, tk)
        acc_ref[...] += jnp.dot(vt_ref[:, pl.ds(off, tk)], p_ref[slot], preferred_element_type=F32)

    unroll = min(ATT_UNROLL, nk)

    def group(jj, carry):
        j = unroll * jj
        for u in range(unroll):
            probs(j + u, u % 2)
            accumulate(jnp.maximum(j + u - 1, 0), (u + 1) % 2)
        return carry

    lax.fori_loop(0, nk // unroll, group, 0)
    accumulate(nk - 1, (nk - 1) % 2)
    _attn_finish(acc_ref, l_ref, sz_ref, o_ref, tq)


def _attn_online_kernel(q_ref, k_ref, vt_ref, sz_ref, o_ref, qt_ref, s_ref, mx_ref, acc_ref, l_ref, m_ref, *, tk):
    S = k_ref.shape[0]
    tq = q_ref.shape[0]
    nk = S // tk
    _load_qt(q_ref, qt_ref, tq)
    acc_ref[...] = jnp.zeros_like(acc_ref)
    l_ref[...] = jnp.zeros_like(l_ref)
    m_ref[...] = jnp.full_like(m_ref, NEG_BIG)

    def scores(j, slot):
        off = pl.multiple_of(j * tk, tk)
        s = jnp.dot(k_ref[pl.ds(off, tk), :], qt_ref[...], preferred_element_type=F32)
        s_ref[slot] = s
        mx_ref[slot] = jnp.max(s, axis=0, keepdims=True)

    def accumulate(j, slot):
        off = pl.multiple_of(j * tk, tk)
        m_old = m_ref[...]
        m_new = jnp.maximum(m_old, mx_ref[slot])
        alpha = jnp.exp2(m_old - m_new)
        p = jnp.exp2(s_ref[slot] - m_new)
        l_ref[...] = l_ref[...] * alpha + _sublane_partial_sums(p)
        pv = jnp.dot(vt_ref[:, pl.ds(off, tk)], p.astype(BF16), preferred_element_type=F32)
        acc_ref[...] = acc_ref[...] * alpha + pv
        m_ref[...] = m_new

    scores(0, 0)

    def pair(jj, carry):
        j = 2 * jj
        scores(j + 1, 1)
        accumulate(j, 0)
        scores(jnp.minimum(j + 2, nk - 1), 0)
        accumulate(j + 1, 1)
        return carry

    lax.fori_loop(0, nk // 2, pair, 0)
    _attn_finish(acc_ref, l_ref, sz_ref, o_ref, tq)


def _attention(qb, kb, vt, szb, score_bound, *, tq, tk):
    B, S, _ = qb.shape
    nk = S // tk
    assert S % tq == 0 and nk % 2 == 0 and nk % min(ATT_UNROLL, nk) == 0
    gw = ATT_GROUP * ATT_HD
    n = ATT_GROUP * tq

    def call(body, scratch, name):
        q_spec = pl.BlockSpec((None, tq, gw), lambda b, kh, i: (b, i, kh))
        return pl.pallas_call(
            functools.partial(body, tk=tk),
            grid=(B, ATT_KV_HEADS, S // tq),
            in_specs=[
                q_spec,
                pl.BlockSpec((None, S, ATT_HD), lambda b, kh, i: (b, 0, kh)),
                pl.BlockSpec((None, ATT_HD, S), lambda b, kh, i: (b, kh, 0)),
                q_spec,
            ],
            out_specs=q_spec,
            out_shape=jax.ShapeDtypeStruct((B, S, B_WIDTH), BF16),
            scratch_shapes=scratch,
            compiler_params=pltpu.CompilerParams(
                dimension_semantics=("parallel", "parallel", "arbitrary"), vmem_limit_bytes=VMEM_LIMIT),
            name=name)

    bounded = call(_attn_bounded_kernel, [
        pltpu.VMEM((ATT_HD, n), BF16), pltpu.VMEM((2, tk, n), BF16),
        pltpu.VMEM((ATT_HD, n), F32), pltpu.VMEM((8, n), F32)], "attn_bounded")
    online = call(_attn_online_kernel, [
        pltpu.VMEM((ATT_HD, n), BF16), pltpu.VMEM((2, tk, n), F32), pltpu.VMEM((2, 1, n), F32),
        pltpu.VMEM((ATT_HD, n), F32), pltpu.VMEM((8, n), F32), pltpu.VMEM((1, n), F32)], "attn_online")
    return lax.cond(score_bound < ATT_BOUND_LIMIT, bounded, online, qb, kb, vt, szb)


def _ev_out_kernel(x_ref, ma_ref, mb_ref, w_ref, o_ref):
    acc = jnp.dot(ma_ref[...], w_ref[:A_WIDTH, :], preferred_element_type=F32)
    acc = acc + jnp.dot(mb_ref[...], w_ref[A_WIDTH:, :], preferred_element_type=F32)
    o_ref[...] = x_ref[...] + acc


def _ev_out(x, ma, mb, w, *, tm):
    B, S, D = x.shape
    row = lambda width: pl.BlockSpec((None, tm, width), lambda b, i: (b, i, 0))
    return pl.pallas_call(
        _ev_out_kernel,
        grid=(B, S // tm),
        in_specs=[row(D), row(A_WIDTH), row(B_WIDTH), _const_spec((A_WIDTH + B_WIDTH, D))],
        out_specs=row(D),
        out_shape=jax.ShapeDtypeStruct((B, S, D), F32),
        compiler_params=pltpu.CompilerParams(
            dimension_semantics=("parallel", "parallel"), vmem_limit_bytes=VMEM_LIMIT),
        name="ev_out",
    )(x, ma, mb, w)


def _odd_kernel(*refs, seq_len, final):
    if final:
        xp_ref, x_ref, xn_ref, ng_ref, win_ref, pw_ref, pb_ref, ps_ref, wout_ref, fg_ref, o_ref = refs
    else:
        xp_ref, x_ref, xn_ref, ng_ref, win_ref, pw_ref, pb_ref, ps_ref, wout_ref, o_ref = refs
    tm = x_ref.shape[0]
    H = POOL_HALO
    t0 = pl.program_id(1) * tm
    x = x_ref[...]
    xe = jnp.concatenate([x, xp_ref[...], xn_ref[...]], axis=0)
    h = _rms(xe, ng_ref[...]).astype(BF16)
    ue = jnp.dot(h, win_ref[:, :POOL_WIDTH], preferred_element_type=F32)
    z = jnp.dot(h[:tm], win_ref[:, POOL_WIDTH:], preferred_element_type=F32)
    ue_bf = ue.astype(BF16)

    t = t0 + lax.broadcasted_iota(jnp.int32, (tm, 1), 0)
    c = lax.broadcasted_iota(jnp.int32, (1, tm + 2 * H), 1)
    pos = t0 + jnp.where(c < tm, c, jnp.where(c < tm + H, c - tm - H, c - H))
    valid = (pos >= 0) & (pos < seq_len)

    pooled, inv_cnt = [], []
    for g, w in enumerate(POOL_WINDOWS):
        lo = t - w // 2
        hi = lo + w
        band = ((pos >= lo) & (pos < hi) & valid).astype(BF16)
        inv_cnt.append(1.0 / (jnp.minimum(hi, seq_len) - jnp.maximum(lo, 0)).astype(F32))
        pooled.append(jnp.dot(band, ue_bf[:, g * POOL_GW:(g + 1) * POOL_GW], preferred_element_type=F32))
    ms = []
    for g in range(POOL_GROUPS):
        d = (pooled[g] * inv_cnt[g] - ue[:tm, g * POOL_GW:(g + 1) * POOL_GW]).astype(BF16)
        ms.append(jnp.dot(d, pw_ref[g], preferred_element_type=F32))
    m = jnp.concatenate(ms, axis=1) + pb_ref[...]
    m = (m * ps_ref[...] * _silu(z)).astype(BF16)
    y = x + jnp.dot(m, wout_ref[...], preferred_element_type=F32)
    if final:
        y = _rms(y, fg_ref[...])
    o_ref[...] = y


def _odd(x, ng, win, pw, pb, ps, wout, fg, *, tm):
    B, S, D = x.shape
    H = POOL_HALO
    nh = tm // H
    last = S // H - 1
    row = pl.BlockSpec((None, tm, D), lambda b, i: (b, i, 0))
    prev = pl.BlockSpec((None, H, D), lambda b, i: (b, jnp.maximum(i * nh - 1, 0), 0))
    nxt = pl.BlockSpec((None, H, D), lambda b, i: (b, jnp.minimum((i + 1) * nh, last), 0))
    final = fg is not None
    in_specs = [
        prev, row, nxt, _const_spec((1, D)), _const_spec((D, 2 * POOL_WIDTH)),
        _const_spec((POOL_GROUPS, POOL_GW, POOL_GW)), _const_spec((1, POOL_WIDTH)),
        _const_spec((1, POOL_WIDTH)), _const_spec((POOL_WIDTH, D)),
    ]
    args = [x, x, x, ng, win, pw, pb, ps, wout]
    if final:
        in_specs.append(_const_spec((1, D)))
        args.append(fg)
    return pl.pallas_call(
        functools.partial(_odd_kernel, seq_len=S, final=final),
        grid=(B, S // tm),
        in_specs=in_specs,
        out_specs=row,
        out_shape=jax.ShapeDtypeStruct((B, S, D), F32),
        compiler_params=pltpu.CompilerParams(
            dimension_semantics=("parallel", "parallel"), vmem_limit_bytes=VMEM_LIMIT),
        name="odd_final" if final else "odd",
    )(*args)


def _rope_tables(S):
    rows = S // GRID_W
    row = jnp.repeat(jnp.arange(rows, dtype=F32), GRID_W)
    col = jnp.tile(jnp.arange(GRID_W, dtype=F32), rows)
    n_pairs = ATT_HD // 4
    freqs = ROPE_THETA ** (-jnp.arange(n_pairs, dtype=F32) / n_pairs)
    ang = jnp.concatenate([row[:, None] * freqs, col[:, None] * freqs], axis=-1)
    c, s = jnp.cos(ang), jnp.sin(ang)
    return jnp.concatenate([c, c], axis=-1), jnp.concatenate([-s, s], axis=-1)


_PAIR_PERM = np.concatenate([np.arange(0, ATT_HD, 2), np.arange(1, ATT_HD, 2)])


def _even_weights(w_in, gate_w, gate_b, q_norm_g, k_norm_g):
    offs = np.concatenate([[0], np.cumsum(EV_SPLITS)])
    seg = lambda i: w_in[:, offs[i]:offs[i + 1]]
    q_a, k_a, v_a, lr_f, lr_b, z_a, q_b, k_b, v_b, z_b = [seg(i) for i in range(10)]
    D = w_in.shape[0]

    def perm_heads(w, nh):
        return w.reshape(D, nh, ATT_HD)[:, :, _PAIR_PERM].reshape(D, nh * ATT_HD)

    lr = jnp.concatenate([lr_f, lr_b, jnp.zeros((D, LR_PAD - 2 * GLA_LOWRANK), w_in.dtype)], axis=1)
    w = jnp.concatenate(
        [q_a, k_a, v_a, z_a, perm_heads(q_b, ATT_HEADS), perm_heads(k_b, ATT_KV_HEADS), v_b, z_b, lr], axis=1)
    gw = jnp.zeros((LR_PAD, 2 * QK_A), F32)
    gw = gw.at[:GLA_LOWRANK, :QK_A].set(gate_w[0])
    gw = gw.at[GLA_LOWRANK:2 * GLA_LOWRANK, QK_A:].set(gate_w[1])
    gb = jnp.concatenate([gate_b[0], gate_b[1]])[None, :]
    return (w.astype(BF16), gw.astype(BF16), gb.astype(F32),
            q_norm_g[_PAIR_PERM][None, :], k_norm_g[_PAIR_PERM][None, :])


def _prepare(norm_g, final_norm_g, ev_w_in, ev_gla_gate_w, ev_gla_gate_b, ev_gla_norm_g,
             ev_q_norm_g, ev_k_norm_g, ev_w_out, od_w_in, od_pool_w, od_pool_b, od_pool_scale, od_w_out):
    layers = []
    for layer in range(DEPTH):
        i = layer // 2
        ng = norm_g[layer][None, :]
        if layer % 2 == 0:
            bound = (ATT_HD * (ATT_HD ** -0.5) * LOG2E * 1.02
                     * jnp.max(jnp.abs(ev_q_norm_g[i])) * jnp.max(jnp.abs(ev_k_norm_g[i])))
            layers.append((ng, _even_weights(ev_w_in[i], ev_gla_gate_w[i], ev_gla_gate_b[i],
                                             ev_q_norm_g[i], ev_k_norm_g[i]),
                           ev_gla_norm_g[i][None, :], bound, ev_w_out[i].astype(BF16)))
        else:
            fg = final_norm_g[None, :] if layer == DEPTH - 1 else None
            layers.append((ng, od_w_in[i].astype(BF16), od_pool_w[i].astype(BF16), od_pool_b[i][None, :],
                           od_pool_scale[i][None, :], od_w_out[i].astype(BF16), fg))
    return layers


def _run(x, layers, rope):
    B, S, D = x.shape
    cos2, sin2 = rope
    tm_in = min(256, S)
    tm_out = min(512, S)
    tm_odd = min(512, S)
    gla_rows = min(512, S)
    tq = min(1024, S)
    tk = min(512, S)
    for layer, ops in enumerate(layers):
        if layer % 2 == 0:
            ng, (w, gw, gb, qg, kg), gla_g, bound, w_out = ops
            qa, ka, va, la, sza, qb, kb, vt, szb = _ev_in(x, ng, w, gw, gb, qg, kg, cos2, sin2, tm=tm_in)
            o_bwd = _gla(qa, ka, va, la, reverse=True, rows=gla_rows)
            ma = _gla(qa, ka, va, la, reverse=False, rows=gla_rows, extra=(o_bwd, sza, gla_g))
            mb = _attention(qb, kb, vt, szb, bound, tq=tq, tk=tk)
            x = _ev_out(x, ma, mb, w_out, tm=tm_out)
        else:
            ng, win, pw, pb, ps, wout, fg = ops
            x = _odd(x, ng, win, pw, pb, ps, wout, fg, tm=tm_odd)
    return x


def _trunk(x, *weights):
    return _run(x, _prepare(*weights), _rope_tables(x.shape[1]))


def kernel(x_prompt, x_sample, norm_g, final_norm_g, ev_w_in, ev_gla_gate_w, ev_gla_gate_b,
           ev_gla_norm_g, ev_q_norm_g, ev_k_norm_g, ev_w_out, od_w_in, od_pool_w, od_pool_b,
           od_pool_scale, od_w_out):
    layers = _prepare(norm_g, final_norm_g, ev_w_in, ev_gla_gate_w, ev_gla_gate_b, ev_gla_norm_g,
                      ev_q_norm_g, ev_k_norm_g, ev_w_out, od_w_in, od_pool_w, od_pool_b, od_pool_scale,
                      od_w_out)
    rope = _rope_tables(x_prompt.shape[1])
    rope_s = rope if x_sample.shape[1] == x_prompt.shape[1] else _rope_tables(x_sample.shape[1])
    return (_run(x_prompt, layers, rope), _run(x_sample, layers, rope_s))
```

```python
import functools

import numpy as np
import jax
import jax.numpy as jnp
from jax import lax
from jax.experimental import pallas as pl
from jax.experimental.pallas import tpu as pltpu

F32 = jnp.float32
BF16 = jnp.bfloat16

D_MODEL = 1024
DEPTH = 4
GRID_W = 64
EPS = 1e-6

GLA_HEADS = 4
GLA_DK = 128
GLA_DV = 256
GLA_LOWRANK = 16
GLA_TAU = 16.0
GLA_CHUNK = 64
GLA_SUPER = 256

ATT_HEADS = 8
ATT_KV_HEADS = 2
ATT_HD = 128
ATT_GROUP = ATT_HEADS // ATT_KV_HEADS
ROPE_THETA = 10000.0
ATT_BOUND_LIMIT = 60.0
ATT_UNROLL = 8

POOL_WINDOWS = (2, 4, 8, 16)
POOL_GROUPS = 4
POOL_WIDTH = 2 * D_MODEL
POOL_GW = POOL_WIDTH // POOL_GROUPS
POOL_HALO = 8
POOL_SUB = 128
POOL_KWIN = POOL_SUB + 4 * POOL_HALO

A_WIDTH = GLA_HEADS * GLA_DV
B_WIDTH = ATT_HEADS * ATT_HD
QK_A = GLA_HEADS * GLA_DK
KV_B = ATT_KV_HEADS * ATT_HD
EV_SPLITS = (QK_A, QK_A, A_WIDTH, GLA_LOWRANK, GLA_LOWRANK, A_WIDTH, B_WIDTH, KV_B, KV_B, B_WIDTH)

C_QA = 0
C_KA = C_QA + QK_A
C_VA = C_KA + QK_A
C_ZA = C_VA + A_WIDTH
C_QB = C_ZA + A_WIDTH
C_KB = C_QB + B_WIDTH
C_VB = C_KB + KV_B
C_ZB = C_VB + KV_B
C_LR = C_ZB + B_WIDTH
LR_PAD = 128
EV_COLS = C_LR + LR_PAD

LOG2E = 1.4426950408889634
NEG_BIG = -1e30

VMEM_LIMIT = 56 * 1024 * 1024


def _rms(x, g):
    return x * lax.rsqrt(jnp.mean(x * x, axis=-1, keepdims=True) + EPS) * g


def _silu(z):
    return z * jax.nn.sigmoid(z)


def _log_sigmoid(g):
    return jnp.minimum(g, 0.0) - jnp.log(1.0 + jnp.exp(-jnp.abs(g)))


def _const_spec(shape):
    nd = len(shape)
    return pl.BlockSpec(shape, lambda *_: (0,) * nd, pipeline_mode=pl.Buffered(1))


def _ev_in_kernel(x_ref, ng_ref, w_ref, gw_ref, gb_ref, qg_ref, kg_ref, cos_ref, sin_ref,
                  qa_ref, ka_ref, va_ref, la_ref, sza_ref, qb_ref, kb_ref, vt_ref, szb_ref):
    h = _rms(x_ref[...], ng_ref[...]).astype(BF16)

    def proj(c0, width):
        return jnp.dot(h, w_ref[:, c0:c0 + width], preferred_element_type=F32)

    lr = proj(C_LR, LR_PAD).astype(BF16)
    qb = proj(C_QB, B_WIDTH)
    gate = jnp.dot(lr, gw_ref[...], preferred_element_type=F32) + gb_ref[...]
    la_ref[...] = _log_sigmoid(gate) * (1.0 / GLA_TAU)

    cos = cos_ref[...]
    sin = sin_ref[...]

    def norm_rope(xh, g):
        xn = _rms(xh, g)
        return xn * cos + pltpu.roll(xn, ATT_HD // 2, 1) * sin

    qscale = (ATT_HD ** -0.5) * LOG2E
    for hh in range(ATT_HEADS):
        sl = slice(hh * ATT_HD, (hh + 1) * ATT_HD)
        qb_ref[:, sl] = (norm_rope(qb[:, sl], qg_ref[...]) * qscale).astype(BF16)
    kb = proj(C_KB, KV_B)
    for hh in range(ATT_KV_HEADS):
        sl = slice(hh * ATT_HD, (hh + 1) * ATT_HD)
        kb_ref[:, sl] = norm_rope(kb[:, sl], kg_ref[...]).astype(BF16)

    vt_ref[...] = proj(C_VB, KV_B).T.astype(BF16)

    sza_ref[...] = _silu(proj(C_ZA, A_WIDTH)).astype(BF16)
    szb_ref[...] = _silu(proj(C_ZB, B_WIDTH)).astype(BF16)
    qa_ref[...] = (proj(C_QA, QK_A) * (GLA_DK ** -0.5)).astype(BF16)
    ka_ref[...] = proj(C_KA, QK_A).astype(BF16)
    va_ref[...] = proj(C_VA, A_WIDTH).astype(BF16)


def _ev_in(x, ng, w, gw, gb, qg, kg, cos2, sin2, *, tm):
    B, S, D = x.shape
    nt = S // tm
    row = lambda width: pl.BlockSpec((None, tm, width), lambda b, i: (b, i, 0))
    outs = [
        (QK_A, BF16), (QK_A, BF16), (A_WIDTH, BF16), (2 * QK_A, F32), (A_WIDTH, BF16),
        (B_WIDTH, BF16), (KV_B, BF16), None, (B_WIDTH, BF16),
    ]
    out_shape, out_specs = [], []
    for o in outs:
        if o is None:
            out_shape.append(jax.ShapeDtypeStruct((B, KV_B, S), BF16))
            out_specs.append(pl.BlockSpec((None, KV_B, tm), lambda b, i: (b, 0, i)))
        else:
            out_shape.append(jax.ShapeDtypeStruct((B, S, o[0]), o[1]))
            out_specs.append(row(o[0]))
    return pl.pallas_call(
        _ev_in_kernel,
        grid=(B, nt),
        in_specs=[
            row(D), _const_spec((1, D)), _const_spec((D, EV_COLS)), _const_spec((LR_PAD, 2 * QK_A)),
            _const_spec((1, 2 * QK_A)), _const_spec((1, ATT_HD)), _const_spec((1, ATT_HD)),
            pl.BlockSpec((tm, ATT_HD), lambda b, i: (i, 0)), pl.BlockSpec((tm, ATT_HD), lambda b, i: (i, 0)),
        ],
        out_specs=out_specs,
        out_shape=out_shape,
        compiler_params=pltpu.CompilerParams(
            dimension_semantics=("parallel", "parallel"), vmem_limit_bytes=VMEM_LIMIT),
        name="ev_in",
    )(x, ng, w, gw, gb, qg, kg, cos2, sin2)


def _split_hi_lo(a):
    hi = a.astype(BF16)
    lo = (a - hi.astype(F32)).astype(BF16)
    return hi, lo


def _gla_kernel(*refs, reverse, final, n_super):
    if final:
        q_ref, k_ref, v_ref, la_ref, ob_ref, sz_ref, g_ref, o_ref, st_ref = refs
    else:
        q_ref, k_ref, v_ref, la_ref, o_ref, st_ref = refs
    SC, C = GLA_SUPER, GLA_CHUNK
    n_chunk = SC // C

    @pl.when(pl.program_id(1) == 0)
    def _():
        st_ref[...] = jnp.zeros_like(st_ref)

    ri = lax.broadcasted_iota(jnp.int32, (SC, SC), 0)
    ci = lax.broadcasted_iota(jnp.int32, (SC, SC), 1)
    same = (ri // C) == (ci // C)
    if reverse:
        incl = same & (ci >= ri)
        incl_t = same & (ri >= ci)
        strict_t = same & (ri < ci)
    else:
        incl = same & (ci <= ri)
        incl_t = same & (ri <= ci)
        strict_t = same & (ri > ci)
    incl_bf = incl.astype(BF16)
    incl_t_bf = incl_t.astype(BF16)
    strict_t_bf = strict_t.astype(BF16)
    lane_chunk = lax.broadcasted_iota(jnp.int32, (GLA_DK, SC), 1) // C

    order = range(n_super - 1, -1, -1) if reverse else range(n_super)
    corder = range(n_chunk - 1, -1, -1) if reverse else range(n_chunk)
    bodies = [(sc, hd) for sc in order for hd in range(GLA_HEADS)]
    rows = {sc: slice(sc * SC, (sc + 1) * SC) for sc in order}
    kcol = [slice(hd * GLA_DK, (hd + 1) * GLA_DK) for hd in range(GLA_HEADS)]
    vcol = [slice(hd * GLA_DV, (hd + 1) * GLA_DV) for hd in range(GLA_HEADS)]
    dot = functools.partial(jnp.dot, preferred_element_type=F32)

    hl, hl_t, k_t = {}, {}, {}
    for key in bodies:
        sc, hd = key
        la = la_ref[rows[sc], kcol[hd]]
        hi, lo = _split_hi_lo(la)
        hl[key] = jnp.concatenate([hi, lo], axis=1)
        hi_t, lo_t = _split_hi_lo(la.T)
        hl_t[key] = jnp.concatenate([hi_t, lo_t], axis=0)
        k_t[key] = k_ref[rows[sc], kcol[hd]].astype(F32).T
    b2 = {key: dot(incl_bf, hl[key]) for key in bodies}
    bt2 = {key: dot(hl_t[key], incl_t_bf) for key in bodies}
    gt2 = {key: dot(hl_t[key], strict_t_bf) for key in bodies}

    q_dec, k_inv_t, k_end_t, tot_t = {}, {}, {}, {}
    for key in bodies:
        sc, hd = key
        b = b2[key][:, :GLA_DK] + b2[key][:, GLA_DK:]
        b_t = bt2[key][:GLA_DK] + bt2[key][GLA_DK:]
        g_t = gt2[key][:GLA_DK] + gt2[key][GLA_DK:]
        tot_t[key] = b_t + g_t
        q_dec[key] = (q_ref[rows[sc], kcol[hd]].astype(F32) * jnp.exp(b)).astype(BF16)
        k_inv_t[key] = (k_t[key] * jnp.exp(-b_t)).astype(BF16)
        k_end_t[key] = k_t[key] * jnp.exp(g_t)

    scores = {key: dot(q_dec[key], k_inv_t[key]) for key in bodies}
    o_sc = {}
    for key in bodies:
        sc, hd = key
        o_sc[key] = dot(jnp.where(incl, scores[key], 0.0).astype(BF16), v_ref[rows[sc], vcol[hd]])
    kv = {}
    for key in bodies:
        sc, hd = key
        for c in corder:
            k_end_c = jnp.where(lane_chunk == c, k_end_t[key], 0.0).astype(BF16)
            kv[key, c] = dot(k_end_c, v_ref[rows[sc], vcol[hd]])

    inter = {}
    for sc in order:
        for c in corder:
            for hd in range(GLA_HEADS):
                key = (sc, hd)
                state = st_ref[hd]
                inter[key, c] = dot(q_dec[key][c * C:(c + 1) * C, :], state.astype(BF16))
                st_ref[hd] = state * jnp.exp(tot_t[key][:, c * C:c * C + 1]) + kv[key, c]

    for key in bodies:
        sc, hd = key
        o = o_sc[key] + jnp.concatenate([inter[key, c] for c in range(n_chunk)], axis=0)
        if final:
            o = o + ob_ref[rows[sc], vcol[hd]]
            o = _rms(o, g_ref[:, vcol[hd]]) * sz_ref[rows[sc], vcol[hd]].astype(F32)
        o_ref[rows[sc], vcol[hd]] = o.astype(o_ref.dtype)


def _gla(q, k, v, la, *, reverse, rows, extra=None):
    B, S, _ = q.shape
    nb = S // rows
    final = extra is not None
    blk = (lambda i: nb - 1 - i) if reverse else (lambda i: i)
    la_blk = 1 if reverse else 0
    qk_spec = pl.BlockSpec((None, rows, QK_A), lambda b, i: (b, blk(i), 0))
    v_spec = pl.BlockSpec((None, rows, A_WIDTH), lambda b, i: (b, blk(i), 0))
    la_spec = pl.BlockSpec((None, rows, QK_A), lambda b, i: (b, blk(i), la_blk))
    in_specs = [qk_spec, qk_spec, v_spec, la_spec]
    args = [q, k, v, la]
    if final:
        ob, sz, g = extra
        in_specs += [v_spec, v_spec, _const_spec((1, A_WIDTH))]
        args += [ob, sz, g]
    return pl.pallas_call(
        functools.partial(_gla_kernel, reverse=reverse, final=final, n_super=rows // GLA_SUPER),
        grid=(B, nb),
        in_specs=in_specs,
        out_specs=v_spec,
        out_shape=jax.ShapeDtypeStruct((B, S, A_WIDTH), BF16 if final else F32),
        scratch_shapes=[pltpu.VMEM((GLA_HEADS, GLA_DK, GLA_DV), F32)],
        compiler_params=pltpu.CompilerParams(
            dimension_semantics=("parallel", "arbitrary"), vmem_limit_bytes=VMEM_LIMIT),
        name="gla_fwd" if final else "gla_bwd",
    )(*args)


def _load_qt(q_ref, qt_ref, tq):
    for hh in range(ATT_GROUP):
        qt_ref[:, hh * tq:(hh + 1) * tq] = q_ref[:, hh * ATT_HD:(hh + 1) * ATT_HD].astype(F32).T.astype(BF16)


def _attn_finish(acc_ref, l_ref, sz_ref, o_ref, tq):
    o_t = acc_ref[...] * (1.0 / jnp.sum(l_ref[...], axis=0, keepdims=True))
    for hh in range(ATT_GROUP):
        sl = slice(hh * ATT_HD, (hh + 1) * ATT_HD)
        o_ref[:, sl] = (o_t[:, hh * tq:(hh + 1) * tq].T * sz_ref[:, sl].astype(F32)).astype(o_ref.dtype)


def _sublane_partial_sums(p):
    return p.reshape(p.shape[0] // 8, 8, p.shape[1]).sum(axis=0)


def _attn_bounded_kernel(q_ref, k_ref, vt_ref, sz_ref, o_ref, qt_ref, p_ref, acc_ref, l_ref, *, tk):
    S = k_ref.shape[0]
    tq = q_ref.shape[0]
    nk = S // tk
    _load_qt(q_ref, qt_ref, tq)
    acc_ref[...] = jnp.zeros_like(acc_ref)
    l_ref[...] = jnp.zeros_like(l_ref)
    p_ref[1] = jnp.zeros_like(p_ref[1])

    def probs(j, slot):
        off = pl.multiple_of(j * tk, tk)
        s = jnp.dot(k_ref[pl.ds(off, tk), :], qt_ref[...], preferred_element_type=F32)
        p = jnp.exp2(s)
        l_ref[...] += _sublane_partial_sums(p)
        p_ref[slot] = p.astype(BF16)

    def accumulate(j, slot):
        off = pl.multiple_of(j * tk, tk)
        acc_ref[...] += jnp.dot(vt_ref[:, pl.ds(off, tk)], p_ref[slot], preferred_element_type=F32)

    unroll = min(ATT_UNROLL, nk)

    def group(jj, carry):
        j = unroll * jj
        for u in range(unroll):
            probs(j + u, u % 2)
            accumulate(jnp.maximum(j + u - 1, 0), (u + 1) % 2)
        return carry

    lax.fori_loop(0, nk // unroll, group, 0)
    accumulate(nk - 1, (nk - 1) % 2)
    _attn_finish(acc_ref, l_ref, sz_ref, o_ref, tq)


def _attn_online_kernel(q_ref, k_ref, vt_ref, sz_ref, o_ref, qt_ref, s_ref, mx_ref, acc_ref, l_ref, m_ref, *, tk):
    S = k_ref.shape[0]
    tq = q_ref.shape[0]
    nk = S // tk
    _load_qt(q_ref, qt_ref, tq)
    acc_ref[...] = jnp.zeros_like(acc_ref)
    l_ref[...] = jnp.zeros_like(l_ref)
    m_ref[...] = jnp.full_like(m_ref, NEG_BIG)

    def scores(j, slot):
        off = pl.multiple_of(j * tk, tk)
        s = jnp.dot(k_ref[pl.ds(off, tk), :], qt_ref[...], preferred_element_type=F32)
        s_ref[slot] = s
        mx_ref[slot] = jnp.max(s, axis=0, keepdims=True)

    def accumulate(j, slot):
        off = pl.multiple_of(j * tk, tk)
        m_old = m_ref[...]
        m_new = jnp.maximum(m_old, mx_ref[slot])
        alpha = jnp.exp2(m_old - m_new)
        p = jnp.exp2(s_ref[slot] - m_new)
        l_ref[...] = l_ref[...] * alpha + _sublane_partial_sums(p)
        pv = jnp.dot(vt_ref[:, pl.ds(off, tk)], p.astype(BF16), preferred_element_type=F32)
        acc_ref[...] = acc_ref[...] * alpha + pv
        m_ref[...] = m_new

    scores(0, 0)

    def pair(jj, carry):
        j = 2 * jj
        scores(j + 1, 1)
        accumulate(j, 0)
        scores(jnp.minimum(j + 2, nk - 1), 0)
        accumulate(j + 1, 1)
        return carry

    lax.fori_loop(0, nk // 2, pair, 0)
    _attn_finish(acc_ref, l_ref, sz_ref, o_ref, tq)


def _attention(qb, kb, vt, szb, score_bound, *, tq, tk):
    B, S, _ = qb.shape
    nk = S // tk
    assert S % tq == 0 and nk % 2 == 0 and nk % min(ATT_UNROLL, nk) == 0
    gw = ATT_GROUP * ATT_HD
    n = ATT_GROUP * tq

    def call(body, scratch, name):
        q_spec = pl.BlockSpec((None, tq, gw), lambda b, kh, i: (b, i, kh))
        return pl.pallas_call(
            functools.partial(body, tk=tk),
            grid=(B, ATT_KV_HEADS, S // tq),
            in_specs=[
                q_spec,
                pl.BlockSpec((None, S, ATT_HD), lambda b, kh, i: (b, 0, kh)),
                pl.BlockSpec((None, ATT_HD, S), lambda b, kh, i: (b, kh, 0)),
                q_spec,
            ],
            out_specs=q_spec,
            out_shape=jax.ShapeDtypeStruct((B, S, B_WIDTH), BF16),
            scratch_shapes=scratch,
            compiler_params=pltpu.CompilerParams(
                dimension_semantics=("parallel", "parallel", "arbitrary"), vmem_limit_bytes=VMEM_LIMIT),
            name=name)

    bounded = call(_attn_bounded_kernel, [
        pltpu.VMEM((ATT_HD, n), BF16), pltpu.VMEM((2, tk, n), BF16),
        pltpu.VMEM((ATT_HD, n), F32), pltpu.VMEM((8, n), F32)], "attn_bounded")
    online = call(_attn_online_kernel, [
        pltpu.VMEM((ATT_HD, n), BF16), pltpu.VMEM((2, tk, n), F32), pltpu.VMEM((2, 1, n), F32),
        pltpu.VMEM((ATT_HD, n), F32), pltpu.VMEM((8, n), F32), pltpu.VMEM((1, n), F32)], "attn_online")
    return lax.cond(score_bound < ATT_BOUND_LIMIT, bounded, online, qb, kb, vt, szb)


def _ev_out_kernel(x_ref, ma_ref, mb_ref, w_ref, o_ref):
    acc = jnp.dot(ma_ref[...], w_ref[:A_WIDTH, :], preferred_element_type=F32)
    acc = acc + jnp.dot(mb_ref[...], w_ref[A_WIDTH:, :], preferred_element_type=F32)
    o_ref[...] = x_ref[...] + acc


def _ev_out(x, ma, mb, w, *, tm):
    B, S, D = x.shape
    row = lambda width: pl.BlockSpec((None, tm, width), lambda b, i: (b, i, 0))
    return pl.pallas_call(
        _ev_out_kernel,
        grid=(B, S // tm),
        in_specs=[row(D), row(A_WIDTH), row(B_WIDTH), _const_spec((A_WIDTH + B_WIDTH, D))],
        out_specs=row(D),
        out_shape=jax.ShapeDtypeStruct((B, S, D), F32),
        compiler_params=pltpu.CompilerParams(
            dimension_semantics=("parallel", "parallel"), vmem_limit_bytes=VMEM_LIMIT),
        name="ev_out",
    )(x, ma, mb, w)


def _odd_kernel(*refs, seq_len, final):
    if final:
        xp_ref, x_ref, xn_ref, ng_ref, win_ref, pw_ref, pb_ref, ps_ref, wout_ref, fg_ref, o_ref = refs
    else:
        xp_ref, x_ref, xn_ref, ng_ref, win_ref, pw_ref, pb_ref, ps_ref, wout_ref, o_ref = refs
    tm = x_ref.shape[0]
    H = POOL_HALO
    t0 = pl.program_id(1) * tm
    x = x_ref[...]
    xe = jnp.concatenate([x, xp_ref[...], xn_ref[...]], axis=0)
    h = _rms(xe, ng_ref[...]).astype(BF16)
    ue = jnp.dot(h, win_ref[:, :POOL_WIDTH], preferred_element_type=F32)
    z = jnp.dot(h[:tm], win_ref[:, POOL_WIDTH:], preferred_element_type=F32)
    ue_bf = ue.astype(BF16)

    n_sub = tm // POOL_SUB
    pooled = [[None] * n_sub for _ in range(POOL_GROUPS)]
    inv_cnt = [[None] * n_sub for _ in range(POOL_GROUPS)]
    c = lax.broadcasted_iota(jnp.int32, (1, POOL_KWIN + 2 * H), 1)
    for sb in range(n_sub):
        k0 = min(max(sb * POOL_SUB - 2 * H, 0), tm - POOL_KWIN)
        t = t0 + sb * POOL_SUB + lax.broadcasted_iota(jnp.int32, (POOL_SUB, 1), 0)
        pos = t0 + jnp.where(c < POOL_KWIN, k0 + c,
                             jnp.where(c < POOL_KWIN + H, c - POOL_KWIN - H, c - POOL_KWIN - H + tm))
        valid = (pos >= 0) & (pos < seq_len)
        keys = jnp.concatenate([ue_bf[k0:k0 + POOL_KWIN], ue_bf[tm:]], axis=0)
        for g, w in enumerate(POOL_WINDOWS):
            lo = t - w // 2
            hi = lo + w
            band = ((pos >= lo) & (pos < hi) & valid).astype(BF16)
            inv_cnt[g][sb] = 1.0 / (jnp.minimum(hi, seq_len) - jnp.maximum(lo, 0)).astype(F32)
            pooled[g][sb] = jnp.dot(band, keys[:, g * POOL_GW:(g + 1) * POOL_GW], preferred_element_type=F32)
    pooled = [jnp.concatenate(pg, axis=0) for pg in pooled]
    inv_cnt = [jnp.concatenate(ig, axis=0) for ig in inv_cnt]
    ms = []
    for g in range(POOL_GROUPS):
        d = (pooled[g] * inv_cnt[g] - ue[:tm, g * POOL_GW:(g + 1) * POOL_GW]).astype(BF16)
        ms.append(jnp.dot(d, pw_ref[g], preferred_element_type=F32))
    m = jnp.concatenate(ms, axis=1) + pb_ref[...]
    m = (m * ps_ref[...] * _silu(z)).astype(BF16)
    y = x + jnp.dot(m, wout_ref[...], preferred_element_type=F32)
    if final:
        y = _rms(y, fg_ref[...])
    o_ref[...] = y


def _odd(x, ng, win, pw, pb, ps, wout, fg, *, tm):
    B, S, D = x.shape
    H = POOL_HALO
    nh = tm // H
    last = S // H - 1
    row = pl.BlockSpec((None, tm, D), lambda b, i: (b, i, 0))
    prev = pl.BlockSpec((None, H, D), lambda b, i: (b, jnp.maximum(i * nh - 1, 0), 0))
    nxt = pl.BlockSpec((None, H, D), lambda b, i: (b, jnp.minimum((i + 1) * nh, last), 0))
    final = fg is not None
    in_specs = [
        prev, row, nxt, _const_spec((1, D)), _const_spec((D, 2 * POOL_WIDTH)),
        _const_spec((POOL_GROUPS, POOL_GW, POOL_GW)), _const_spec((1, POOL_WIDTH)),
        _const_spec((1, POOL_WIDTH)), _const_spec((POOL_WIDTH, D)),
    ]
    args = [x, x, x, ng, win, pw, pb, ps, wout]
    if final:
        in_specs.append(_const_spec((1, D)))
        args.append(fg)
    return pl.pallas_call(
        functools.partial(_odd_kernel, seq_len=S, final=final),
        grid=(B, S // tm),
        in_specs=in_specs,
        out_specs=row,
        out_shape=jax.ShapeDtypeStruct((B, S, D), F32),
        compiler_params=pltpu.CompilerParams(
            dimension_semantics=("parallel", "parallel"), vmem_limit_bytes=VMEM_LIMIT),
        name="odd_final" if final else "odd",
    )(*args)


def _rope_tables(S):
    rows = S // GRID_W
    row = jnp.repeat(jnp.arange(rows, dtype=F32), GRID_W)
    col = jnp.tile(jnp.arange(GRID_W, dtype=F32), rows)
    n_pairs = ATT_HD // 4
    freqs = ROPE_THETA ** (-jnp.arange(n_pairs, dtype=F32) / n_pairs)
    ang = jnp.concatenate([row[:, None] * freqs, col[:, None] * freqs], axis=-1)
    c, s = jnp.cos(ang), jnp.sin(ang)
    return jnp.concatenate([c, c], axis=-1), jnp.concatenate([-s, s], axis=-1)


_PAIR_PERM = np.concatenate([np.arange(0, ATT_HD, 2), np.arange(1, ATT_HD, 2)])


def _even_weights(w_in, gate_w, gate_b, q_norm_g, k_norm_g):
    offs = np.concatenate([[0], np.cumsum(EV_SPLITS)])
    seg = lambda i: w_in[:, offs[i]:offs[i + 1]]
    q_a, k_a, v_a, lr_f, lr_b, z_a, q_b, k_b, v_b, z_b = [seg(i) for i in range(10)]
    D = w_in.shape[0]

    def perm_heads(w, nh):
        return w.reshape(D, nh, ATT_HD)[:, :, _PAIR_PERM].reshape(D, nh * ATT_HD)

    lr = jnp.concatenate([lr_f, lr_b, jnp.zeros((D, LR_PAD - 2 * GLA_LOWRANK), w_in.dtype)], axis=1)
    w = jnp.concatenate(
        [q_a, k_a, v_a, z_a, perm_heads(q_b, ATT_HEADS), perm_heads(k_b, ATT_KV_HEADS), v_b, z_b, lr], axis=1)
    gw = jnp.zeros((LR_PAD, 2 * QK_A), F32)
    gw = gw.at[:GLA_LOWRANK, :QK_A].set(gate_w[0])
    gw = gw.at[GLA_LOWRANK:2 * GLA_LOWRANK, QK_A:].set(gate_w[1])
    gb = jnp.concatenate([gate_b[0], gate_b[1]])[None, :]
    return (w.astype(BF16), gw.astype(BF16), gb.astype(F32),
            q_norm_g[_PAIR_PERM][None, :], k_norm_g[_PAIR_PERM][None, :])


def _prepare(norm_g, final_norm_g, ev_w_in, ev_gla_gate_w, ev_gla_gate_b, ev_gla_norm_g,
             ev_q_norm_g, ev_k_norm_g, ev_w_out, od_w_in, od_pool_w, od_pool_b, od_pool_scale, od_w_out):
    layers = []
    for layer in range(DEPTH):
        i = layer // 2
        ng = norm_g[layer][None, :]
        if layer % 2 == 0:
            bound = (ATT_HD * (ATT_HD ** -0.5) * LOG2E * 1.02
                     * jnp.max(jnp.abs(ev_q_norm_g[i])) * jnp.max(jnp.abs(ev_k_norm_g[i])))
            layers.append((ng, _even_weights(ev_w_in[i], ev_gla_gate_w[i], ev_gla_gate_b[i],
                                             ev_q_norm_g[i], ev_k_norm_g[i]),
                           ev_gla_norm_g[i][None, :], bound, ev_w_out[i].astype(BF16)))
        else:
            fg = final_norm_g[None, :] if layer == DEPTH - 1 else None
            layers.append((ng, od_w_in[i].astype(BF16), od_pool_w[i].astype(BF16), od_pool_b[i][None, :],
                           od_pool_scale[i][None, :], od_w_out[i].astype(BF16), fg))
    return layers


def _run(x, layers, rope):
    B, S, D = x.shape
    cos2, sin2 = rope
    tm_in = min(256, S)
    tm_out = min(512, S)
    tm_odd = min(512, S)
    gla_rows = min(512, S)
    tq = min(1024, S)
    tk = min(512, S)
    for layer, ops in enumerate(layers):
        if layer % 2 == 0:
            ng, (w, gw, gb, qg, kg), gla_g, bound, w_out = ops
            qa, ka, va, la, sza, qb, kb, vt, szb = _ev_in(x, ng, w, gw, gb, qg, kg, cos2, sin2, tm=tm_in)
            o_bwd = _gla(qa, ka, va, la, reverse=True, rows=gla_rows)
            ma = _gla(qa, ka, va, la, reverse=False, rows=gla_rows, extra=(o_bwd, sza, gla_g))
            mb = _attention(qb, kb, vt, szb, bound, tq=tq, tk=tk)
            x = _ev_out(x, ma, mb, w_out, tm=tm_out)
        else:
            ng, win, pw, pb, ps, wout, fg = ops
            x = _odd(x, ng, win, pw, pb, ps, wout, fg, tm=tm_odd)
    return x


def _trunk(x, *weights):
    return _run(x, _prepare(*weights), _rope_tables(x.shape[1]))


def kernel(x_prompt, x_sample, norm_g, final_norm_g, ev_w_in, ev_gla_gate_w, ev_gla_gate_b,
           ev_gla_norm_g, ev_q_norm_g, ev_k_norm_g, ev_w_out, od_w_in, od_pool_w, od_pool_b,
           od_pool_scale, od_w_out):
    layers = _prepare(norm_g, final_norm_g, ev_w_in, ev_gla_gate_w, ev_gla_gate_b, ev_gla_norm_g,
                      ev_q_norm_g, ev_k_norm_g, ev_w_out, od_w_in, od_pool_w, od_pool_b, od_pool_scale,
                      od_w_out)
    rope = _rope_tables(x_prompt.shape[1])
    rope_s = rope if x_sample.shape[1] == x_prompt.shape[1] else _rope_tables(x_sample.shape[1])
    return (_run(x_prompt, layers, rope), _run(x_sample, layers, rope_s))
```

```python
import functools

import numpy as np
import jax
import jax.numpy as jnp
from jax import lax
from jax.experimental import pallas as pl
from jax.experimental.pallas import tpu as pltpu

F32 = jnp.float32
BF16 = jnp.bfloat16

D_MODEL = 1024
DEPTH = 4
GRID_W = 64
EPS = 1e-6

GLA_HEADS = 4
GLA_DK = 128
GLA_DV = 256
GLA_LOWRANK = 16
GLA_TAU = 16.0
GLA_CHUNK = 64
GLA_SUPER = 256

ATT_HEADS = 8
ATT_KV_HEADS = 2
ATT_HD = 128
ATT_GROUP = ATT_HEADS // ATT_KV_HEADS
ROPE_THETA = 10000.0
ATT_BOUND_LIMIT = 60.0
ATT_UNROLL = 8

POOL_WINDOWS = (2, 4, 8, 16)
POOL_GROUPS = 4
POOL_WIDTH = 2 * D_MODEL
POOL_GW = POOL_WIDTH // POOL_GROUPS
POOL_HALO = 8
POOL_SUB = 128
POOL_KWIN = POOL_SUB + 4 * POOL_HALO

A_WIDTH = GLA_HEADS * GLA_DV
B_WIDTH = ATT_HEADS * ATT_HD
QK_A = GLA_HEADS * GLA_DK
KV_B = ATT_KV_HEADS * ATT_HD
EV_SPLITS = (QK_A, QK_A, A_WIDTH, GLA_LOWRANK, GLA_LOWRANK, A_WIDTH, B_WIDTH, KV_B, KV_B, B_WIDTH)

C_QA = 0
C_KA = C_QA + QK_A
C_VA = C_KA + QK_A
C_ZA = C_VA + A_WIDTH
C_QB = C_ZA + A_WIDTH
C_KB = C_QB + B_WIDTH
C_VB = C_KB + KV_B
C_ZB = C_VB + KV_B
C_LR = C_ZB + B_WIDTH
LR_PAD = 128
EV_COLS = C_LR + LR_PAD

LOG2E = 1.4426950408889634
NEG_BIG = -1e30

VMEM_LIMIT = 56 * 1024 * 1024


def _rms(x, g):
    return x * lax.rsqrt(jnp.mean(x * x, axis=-1, keepdims=True) + EPS) * g


def _silu(z):
    return z * jax.nn.sigmoid(z)


def _log_sigmoid(g):
    return jnp.minimum(g, 0.0) - jnp.log(1.0 + jnp.exp(-jnp.abs(g)))


def _const_spec(shape):
    nd = len(shape)
    return pl.BlockSpec(shape, lambda *_: (0,) * nd, pipeline_mode=pl.Buffered(1))


def _ev_in_kernel(x_ref, ng_ref, w_ref, gw_ref, gb_ref, qg_ref, kg_ref, cos_ref, sin_ref,
                  qa_ref, ka_ref, va_ref, la_ref, sza_ref, qb_ref, kb_ref, vt_ref, szb_ref):
    h = _rms(x_ref[...], ng_ref[...]).astype(BF16)

    def proj(c0, width):
        return jnp.dot(h, w_ref[:, c0:c0 + width], preferred_element_type=F32)

    lr = proj(C_LR, LR_PAD).astype(BF16)
    qb = proj(C_QB, B_WIDTH)
    gate = jnp.dot(lr, gw_ref[...], preferred_element_type=F32) + gb_ref[...]
    la_ref[...] = _log_sigmoid(gate) * (1.0 / GLA_TAU)

    cos = cos_ref[...]
    sin = sin_ref[...]

    def norm_rope(xh, g):
        xn = _rms(xh, g)
        return xn * cos + pltpu.roll(xn, ATT_HD // 2, 1) * sin

    qscale = (ATT_HD ** -0.5) * LOG2E
    for hh in range(ATT_HEADS):
        sl = slice(hh * ATT_HD, (hh + 1) * ATT_HD)
        qb_ref[:, sl] = (norm_rope(qb[:, sl], qg_ref[...]) * qscale).astype(BF16)
    kb = proj(C_KB, KV_B)
    for hh in range(ATT_KV_HEADS):
        sl = slice(hh * ATT_HD, (hh + 1) * ATT_HD)
        kb_ref[:, sl] = norm_rope(kb[:, sl], kg_ref[...]).astype(BF16)

    vt_ref[...] = proj(C_VB, KV_B).T.astype(BF16)

    sza_ref[...] = _silu(proj(C_ZA, A_WIDTH)).astype(BF16)
    szb_ref[...] = _silu(proj(C_ZB, B_WIDTH)).astype(BF16)
    qa_ref[...] = (proj(C_QA, QK_A) * (GLA_DK ** -0.5)).astype(BF16)
    ka_ref[...] = proj(C_KA, QK_A).astype(BF16)
    va_ref[...] = proj(C_VA, A_WIDTH).astype(BF16)


def _ev_in(x, ng, w, gw, gb, qg, kg, cos2, sin2, *, tm):
    B, S, D = x.shape
    nt = S // tm
    row = lambda width: pl.BlockSpec((None, tm, width), lambda b, i: (b, i, 0))
    outs = [
        (QK_A, BF16), (QK_A, BF16), (A_WIDTH, BF16), (2 * QK_A, F32), (A_WIDTH, BF16),
        (B_WIDTH, BF16), (KV_B, BF16), None, (B_WIDTH, BF16),
    ]
    out_shape, out_specs = [], []
    for o in outs:
        if o is None:
            out_shape.append(jax.ShapeDtypeStruct((B, KV_B, S), BF16))
            out_specs.append(pl.BlockSpec((None, KV_B, tm), lambda b, i: (b, 0, i)))
        else:
            out_shape.append(jax.ShapeDtypeStruct((B, S, o[0]), o[1]))
            out_specs.append(row(o[0]))
    return pl.pallas_call(
        _ev_in_kernel,
        grid=(B, nt),
        in_specs=[
            row(D), _const_spec((1, D)), _const_spec((D, EV_COLS)), _const_spec((LR_PAD, 2 * QK_A)),
            _const_spec((1, 2 * QK_A)), _const_spec((1, ATT_HD)), _const_spec((1, ATT_HD)),
            pl.BlockSpec((tm, ATT_HD), lambda b, i: (i, 0)), pl.BlockSpec((tm, ATT_HD), lambda b, i: (i, 0)),
        ],
        out_specs=out_specs,
        out_shape=out_shape,
        compiler_params=pltpu.CompilerParams(
            dimension_semantics=("parallel", "parallel"), vmem_limit_bytes=VMEM_LIMIT),
        name="ev_in",
    )(x, ng, w, gw, gb, qg, kg, cos2, sin2)


def _split_hi_lo(a):
    hi = a.astype(BF16)
    lo = (a - hi.astype(F32)).astype(BF16)
    return hi, lo


def _gla_kernel(*refs, reverse, final, n_super):
    if final:
        q_ref, k_ref, v_ref, la_ref, ob_ref, sz_ref, g_ref, o_ref, st_ref = refs
    else:
        q_ref, k_ref, v_ref, la_ref, o_ref, st_ref = refs
    SC, C = GLA_SUPER, GLA_CHUNK
    n_chunk = SC // C

    @pl.when(pl.program_id(1) == 0)
    def _():
        st_ref[...] = jnp.zeros_like(st_ref)

    ri = lax.broadcasted_iota(jnp.int32, (SC, SC), 0)
    ci = lax.broadcasted_iota(jnp.int32, (SC, SC), 1)
    same = (ri // C) == (ci // C)
    if reverse:
        incl = same & (ci >= ri)
        incl_t = same & (ri >= ci)
        strict_t = same & (ri < ci)
    else:
        incl = same & (ci <= ri)
        incl_t = same & (ri <= ci)
        strict_t = same & (ri > ci)
    incl_bf = incl.astype(BF16)
    incl_t_bf = incl_t.astype(BF16)
    strict_t_bf = strict_t.astype(BF16)
    lane_chunk = lax.broadcasted_iota(jnp.int32, (GLA_DK, SC), 1) // C

    order = range(n_super - 1, -1, -1) if reverse else range(n_super)
    corder = range(n_chunk - 1, -1, -1) if reverse else range(n_chunk)
    bodies = [(sc, hd) for sc in order for hd in range(GLA_HEADS)]
    rows = {sc: slice(sc * SC, (sc + 1) * SC) for sc in order}
    kcol = [slice(hd * GLA_DK, (hd + 1) * GLA_DK) for hd in range(GLA_HEADS)]
    vcol = [slice(hd * GLA_DV, (hd + 1) * GLA_DV) for hd in range(GLA_HEADS)]
    dot = functools.partial(jnp.dot, preferred_element_type=F32)

    hl, hl_t, k_t = {}, {}, {}
    for key in bodies:
        sc, hd = key
        la = la_ref[rows[sc], kcol[hd]]
        hi, lo = _split_hi_lo(la)
        hl[key] = jnp.concatenate([hi, lo], axis=1)
        hi_t, lo_t = _split_hi_lo(la.T)
        hl_t[key] = jnp.concatenate([hi_t, lo_t], axis=0)
        k_t[key] = k_ref[rows[sc], kcol[hd]].astype(F32).T
    b2 = {key: dot(incl_bf, hl[key]) for key in bodies}
    bt2 = {key: dot(hl_t[key], incl_t_bf) for key in bodies}
    gt2 = {key: dot(hl_t[key], strict_t_bf) for key in bodies}

    q_dec, k_inv_t, k_end_t, tot_t = {}, {}, {}, {}
    for key in bodies:
        sc, hd = key
        b = b2[key][:, :GLA_DK] + b2[key][:, GLA_DK:]
        b_t = bt2[key][:GLA_DK] + bt2[key][GLA_DK:]
        g_t = gt2[key][:GLA_DK] + gt2[key][GLA_DK:]
        tot_t[key] = b_t + g_t
        q_dec[key] = (q_ref[rows[sc], kcol[hd]].astype(F32) * jnp.exp(b)).astype(BF16)
        k_inv_t[key] = (k_t[key] * jnp.exp(-b_t)).astype(BF16)
        k_end_t[key] = k_t[key] * jnp.exp(g_t)

    scores = {key: dot(q_dec[key], k_inv_t[key]) for key in bodies}
    o_sc = {}
    for key in bodies:
        sc, hd = key
        o_sc[key] = dot(jnp.where(incl, scores[key], 0.0).astype(BF16), v_ref[rows[sc], vcol[hd]])
    kv = {}
    for key in bodies:
        sc, hd = key
        for c in corder:
            k_end_c = jnp.where(lane_chunk == c, k_end_t[key], 0.0).astype(BF16)
            kv[key, c] = dot(k_end_c, v_ref[rows[sc], vcol[hd]])

    inter = {}
    for sc in order:
        for c in corder:
            for hd in range(GLA_HEADS):
                key = (sc, hd)
                state = st_ref[hd]
                inter[key, c] = dot(q_dec[key][c * C:(c + 1) * C, :], state.astype(BF16))
                st_ref[hd] = state * jnp.exp(tot_t[key][:, c * C:c * C + 1]) + kv[key, c]

    for key in bodies:
        sc, hd = key
        o = o_sc[key] + jnp.concatenate([inter[key, c] for c in range(n_chunk)], axis=0)
        if final:
            o = o + ob_ref[rows[sc], vcol[hd]]
            o = _rms(o, g_ref[:, vcol[hd]]) * sz_ref[rows[sc], vcol[hd]].astype(F32)
        o_ref[rows[sc], vcol[hd]] = o.astype(o_ref.dtype)


def _gla(q, k, v, la, *, reverse, rows, extra=None):
    B, S, _ = q.shape
    nb = S // rows
    final = extra is not None
    blk = (lambda i: nb - 1 - i) if reverse else (lambda i: i)
    la_blk = 1 if reverse else 0
    qk_spec = pl.BlockSpec((None, rows, QK_A), lambda b, i: (b, blk(i), 0))
    v_spec = pl.BlockSpec((None, rows, A_WIDTH), lambda b, i: (b, blk(i), 0))
    la_spec = pl.BlockSpec((None, rows, QK_A), lambda b, i: (b, blk(i), la_blk))
    in_specs = [qk_spec, qk_spec, v_spec, la_spec]
    args = [q, k, v, la]
    if final:
        ob, sz, g = extra
        in_specs += [v_spec, v_spec, _const_spec((1, A_WIDTH))]
        args += [ob, sz, g]
    return pl.pallas_call(
        functools.partial(_gla_kernel, reverse=reverse, final=final, n_super=rows // GLA_SUPER),
        grid=(B, nb),
        in_specs=in_specs,
        out_specs=v_spec,
        out_shape=jax.ShapeDtypeStruct((B, S, A_WIDTH), BF16 if final else F32),
        scratch_shapes=[pltpu.VMEM((GLA_HEADS, GLA_DK, GLA_DV), F32)],
        compiler_params=pltpu.CompilerParams(
            dimension_semantics=("parallel", "arbitrary"), vmem_limit_bytes=VMEM_LIMIT),
        name="gla_fwd" if final else "gla_bwd",
    )(*args)


def _load_qt(q_ref, qt_ref, tq):
    for hh in range(ATT_GROUP):
        qt_ref[:, hh * tq:(hh + 1) * tq] = q_ref[:, hh * ATT_HD:(hh + 1) * ATT_HD].astype(F32).T.astype(BF16)


def _attn_finish(acc_ref, l_ref, sz_ref, o_ref, tq):
    o_t = acc_ref[...] * (1.0 / jnp.sum(l_ref[...], axis=0, keepdims=True))
    for hh in range(ATT_GROUP):
        sl = slice(hh * ATT_HD, (hh + 1) * ATT_HD)
        o_ref[:, sl] = (o_t[:, hh * tq:(hh + 1) * tq].T * sz_ref[:, sl].astype(F32)).astype(o_ref.dtype)


def _sublane_partial_sums(p):
    return p.reshape(p.shape[0] // 8, 8, p.shape[1]).sum(axis=0)


def _attn_bounded_kernel(q_ref, k_ref, vt_ref, sz_ref, o_ref, qt_ref, p_ref, acc_ref, l_ref, *, tk):
    S = k_ref.shape[0]
    tq = q_ref.shape[0]
    nk = S // tk
    _load_qt(q_ref, qt_ref, tq)
    acc_ref[...] = jnp.zeros_like(acc_ref)
    l_ref[...] = jnp.zeros_like(l_ref)
    p_ref[1] = jnp.zeros_like(p_ref[1])

    def probs(j, slot):
        off = pl.multiple_of(j * tk, tk)
        s = jnp.dot(k_ref[pl.ds(off, tk), :], qt_ref[...], preferred_element_type=F32)
        p = jnp.exp2(s)
        l_ref[...] += _sublane_partial_sums(p)
        p_ref[slot] = p.astype(BF16)

    def accumulate(j, slot):
        off = pl.multiple_of(j * tk, tk)
        acc_ref[...] += jnp.dot(vt_ref[:, pl.ds(off, tk)], p_ref[slot], preferred_element_type=F32)

    unroll = min(ATT_UNROLL, nk)

    def group(jj, carry):
        j = unroll * jj
        for u in range(unroll):
            probs(j + u, u % 2)
            accumulate(jnp.maximum(j + u - 1, 0), (u + 1) % 2)
        return carry

    lax.fori_loop(0, nk // unroll, group, 0)
    accumulate(nk - 1, (nk - 1) % 2)
    _attn_finish(acc_ref, l_ref, sz_ref, o_ref, tq)


def _attn_online_kernel(q_ref, k_ref, vt_ref, sz_ref, o_ref, qt_ref, s_ref, mx_ref, acc_ref, l_ref, m_ref, *, tk):
    S = k_ref.shape[0]
    tq = q_ref.shape[0]
    nk = S // tk
    _load_qt(q_ref, qt_ref, tq)
    acc_ref[...] = jnp.zeros_like(acc_ref)
    l_ref[...] = jnp.zeros_like(l_ref)
    m_ref[...] = jnp.full_like(m_ref, NEG_BIG)

    def scores(j, slot):
        off = pl.multiple_of(j * tk, tk)
        s = jnp.dot(k_ref[pl.ds(off, tk), :], qt_ref[...], preferred_element_type=F32)
        s_ref[slot] = s
        mx_ref[slot] = jnp.max(s, axis=0, keepdims=True)

    def accumulate(j, slot):
        off = pl.multiple_of(j * tk, tk)
        m_old = m_ref[...]
        m_new = jnp.maximum(m_old, mx_ref[slot])
        alpha = jnp.exp2(m_old - m_new)
        p = jnp.exp2(s_ref[slot] - m_new)
        l_ref[...] = l_ref[...] * alpha + _sublane_partial_sums(p)
        pv = jnp.dot(vt_ref[:, pl.ds(off, tk)], p.astype(BF16), preferred_element_type=F32)
        acc_ref[...] = acc_ref[...] * alpha + pv
        m_ref[...] = m_new

    scores(0, 0)

    def pair(jj, carry):
        j = 2 * jj
        scores(j + 1, 1)
        accumulate(j, 0)
        scores(jnp.minimum(j + 2, nk - 1), 0)
        accumulate(j + 1, 1)
        return carry

    lax.fori_loop(0, nk // 2, pair, 0)
    _attn_finish(acc_ref, l_ref, sz_ref, o_ref, tq)


def _attention(qb, kb, vt, szb, score_bound, *, tq, tk):
    B, S, _ = qb.shape
    nk = S // tk
    assert S % tq == 0 and nk % 2 == 0 and nk % min(ATT_UNROLL, nk) == 0
    gw = ATT_GROUP * ATT_HD
    n = ATT_GROUP * tq

    def call(body, scratch, name):
        q_spec = pl.BlockSpec((None, tq, gw), lambda b, kh, i: (b, i, kh))
        return pl.pallas_call(
            functools.partial(body, tk=tk),
            grid=(B, ATT_KV_HEADS, S // tq),
            in_specs=[
                q_spec,
                pl.BlockSpec((None, S, ATT_HD), lambda b, kh, i: (b, 0, kh)),
                pl.BlockSpec((None, ATT_HD, S), lambda b, kh, i: (b, kh, 0)),
                q_spec,
            ],
            out_specs=q_spec,
            out_shape=jax.ShapeDtypeStruct((B, S, B_WIDTH), BF16),
            scratch_shapes=scratch,
            compiler_params=pltpu.CompilerParams(
                dimension_semantics=("parallel", "parallel", "arbitrary"), vmem_limit_bytes=VMEM_LIMIT),
            name=name)

    bounded = call(_attn_bounded_kernel, [
        pltpu.VMEM((ATT_HD, n), BF16), pltpu.VMEM((2, tk, n), BF16),
        pltpu.VMEM((ATT_HD, n), F32), pltpu.VMEM((8, n), F32)], "attn_bounded")
    online = call(_attn_online_kernel, [
        pltpu.VMEM((ATT_HD, n), BF16), pltpu.VMEM((2, tk, n), F32), pltpu.VMEM((2, 1, n), F32),
        pltpu.VMEM((ATT_HD, n), F32), pltpu.VMEM((8, n), F32), pltpu.VMEM((1, n), F32)], "attn_online")
    return lax.cond(score_bound < ATT_BOUND_LIMIT, bounded, online, qb, kb, vt, szb)


def _ev_out_kernel(x_ref, ma_ref, mb_ref, w_ref, o_ref):
    acc = jnp.dot(ma_ref[...], w_ref[:A_WIDTH, :], preferred_element_type=F32)
    acc = acc + jnp.dot(mb_ref[...], w_ref[A_WIDTH:, :], preferred_element_type=F32)
    o_ref[...] = x_ref[...] + acc


def _ev_out(x, ma, mb, w, *, tm):
    B, S, D = x.shape
    row = lambda width: pl.BlockSpec((None, tm, width), lambda b, i: (b, i, 0))
    return pl.pallas_call(
        _ev_out_kernel,
        grid=(B, S // tm),
        in_specs=[row(D), row(A_WIDTH), row(B_WIDTH), _const_spec((A_WIDTH + B_WIDTH, D))],
        out_specs=row(D),
        out_shape=jax.ShapeDtypeStruct((B, S, D), F32),
        compiler_params=pltpu.CompilerParams(
            dimension_semantics=("parallel", "parallel"), vmem_limit_bytes=VMEM_LIMIT),
        name="ev_out",
    )(x, ma, mb, w)


def _odd_kernel(*refs, seq_len, final):
    if final:
        xp_ref, x_ref, xn_ref, ng_ref, win_ref, pw_ref, pb_ref, ps_ref, wout_ref, fg_ref, o_ref = refs
    else:
        xp_ref, x_ref, xn_ref, ng_ref, win_ref, pw_ref, pb_ref, ps_ref, wout_ref, o_ref = refs
    tm = x_ref.shape[0]
    H = POOL_HALO
    t0 = pl.program_id(1) * tm
    x = x_ref[...]
    xe = jnp.concatenate([x, xp_ref[...], xn_ref[...]], axis=0)
    h = _rms(xe, ng_ref[...]).astype(BF16)
    ue = jnp.dot(h, win_ref[:, :POOL_WIDTH], preferred_element_type=F32)
    z = jnp.dot(h[:tm], win_ref[:, POOL_WIDTH:], preferred_element_type=F32)
    ue_bf = ue.astype(BF16)

    n_sub = tm // POOL_SUB
    pooled = [[None] * n_sub for _ in range(POOL_GROUPS)]
    inv_cnt = [[None] * n_sub for _ in range(POOL_GROUPS)]
    c = lax.broadcasted_iota(jnp.int32, (1, POOL_KWIN + 2 * H), 1)
    for sb in range(n_sub):
        k0 = min(max(sb * POOL_SUB - 2 * H, 0), tm - POOL_KWIN)
        t = t0 + sb * POOL_SUB + lax.broadcasted_iota(jnp.int32, (POOL_SUB, 1), 0)
        pos = t0 + jnp.where(c < POOL_KWIN, k0 + c,
                             jnp.where(c < POOL_KWIN + H, c - POOL_KWIN - H, c - POOL_KWIN - H + tm))
        valid = (pos >= 0) & (pos < seq_len)
        keys = jnp.concatenate([ue_bf[k0:k0 + POOL_KWIN], ue_bf[tm:]], axis=0)
        for g, w in enumerate(POOL_WINDOWS):
            lo = t - w // 2
            hi = lo + w
            band = ((pos >= lo) & (pos < hi) & valid).astype(BF16)
            inv_cnt[g][sb] = 1.0 / (jnp.minimum(hi, seq_len) - jnp.maximum(lo, 0)).astype(F32)
            pooled[g][sb] = jnp.dot(band, keys[:, g * POOL_GW:(g + 1) * POOL_GW], preferred_element_type=F32)
    pooled = [jnp.concatenate(pg, axis=0) for pg in pooled]
    inv_cnt = [jnp.concatenate(ig, axis=0) for ig in inv_cnt]
    ms = []
    for g in range(POOL_GROUPS):
        d = (pooled[g] * inv_cnt[g] - ue[:tm, g * POOL_GW:(g + 1) * POOL_GW]).astype(BF16)
        ms.append(jnp.dot(d, pw_ref[g], preferred_element_type=F32))
    m = jnp.concatenate(ms, axis=1) + pb_ref[...]
    m = (m * ps_ref[...] * _silu(z)).astype(BF16)
    y = x + jnp.dot(m, wout_ref[...], preferred_element_type=F32)
    if final:
        y = _rms(y, fg_ref[...])
    o_ref[...] = y


def _odd(x, ng, win, pw, pb, ps, wout, fg, *, tm):
    B, S, D = x.shape
    H = POOL_HALO
    nh = tm // H
    last = S // H - 1
    row = pl.BlockSpec((None, tm, D), lambda b, i: (b, i, 0))
    prev = pl.BlockSpec((None, H, D), lambda b, i: (b, jnp.maximum(i * nh - 1, 0), 0))
    nxt = pl.BlockSpec((None, H, D), lambda b, i: (b, jnp.minimum((i + 1) * nh, last), 0))
    final = fg is not None
    in_specs = [
        prev, row, nxt, _const_spec((1, D)), _const_spec((D, 2 * POOL_WIDTH)),
        _const_spec((POOL_GROUPS, POOL_GW, POOL_GW)), _const_spec((1, POOL_WIDTH)),
        _const_spec((1, POOL_WIDTH)), _const_spec((POOL_WIDTH, D)),
    ]
    args = [x, x, x, ng, win, pw, pb, ps, wout]
    if final:
        in_specs.append(_const_spec((1, D)))
        args.append(fg)
    return pl.pallas_call(
        functools.partial(_odd_kernel, seq_len=S, final=final),
        grid=(B, S // tm),
        in_specs=in_specs,
        out_specs=row,
        out_shape=jax.ShapeDtypeStruct((B, S, D), F32),
        compiler_params=pltpu.CompilerParams(
            dimension_semantics=("parallel", "parallel"), vmem_limit_bytes=VMEM_LIMIT),
        name="odd_final" if final else "odd",
    )(*args)


def _rope_tables(S):
    rows = S // GRID_W
    row = jnp.repeat(jnp.arange(rows, dtype=F32), GRID_W)
    col = jnp.tile(jnp.arange(GRID_W, dtype=F32), rows)
    n_pairs = ATT_HD // 4
    freqs = ROPE_THETA ** (-jnp.arange(n_pairs, dtype=F32) / n_pairs)
    ang = jnp.concatenate([row[:, None] * freqs, col[:, None] * freqs], axis=-1)
    c, s = jnp.cos(ang), jnp.sin(ang)
    return jnp.concatenate([c, c], axis=-1), jnp.concatenate([-s, s], axis=-1)


_PAIR_PERM = np.concatenate([np.arange(0, ATT_HD, 2), np.arange(1, ATT_HD, 2)])


def _even_weights(w_in, gate_w, gate_b, q_norm_g, k_norm_g):
    offs = np.concatenate([[0], np.cumsum(EV_SPLITS)])
    seg = lambda i: w_in[:, offs[i]:offs[i + 1]]
    q_a, k_a, v_a, lr_f, lr_b, z_a, q_b, k_b, v_b, z_b = [seg(i) for i in range(10)]
    D = w_in.shape[0]

    def perm_heads(w, nh):
        return w.reshape(D, nh, ATT_HD)[:, :, _PAIR_PERM].reshape(D, nh * ATT_HD)

    lr = jnp.concatenate([lr_f, lr_b, jnp.zeros((D, LR_PAD - 2 * GLA_LOWRANK), w_in.dtype)], axis=1)
    w = jnp.concatenate(
        [q_a, k_a, v_a, z_a, perm_heads(q_b, ATT_HEADS), perm_heads(k_b, ATT_KV_HEADS), v_b, z_b, lr], axis=1)
    gw = jnp.zeros((LR_PAD, 2 * QK_A), F32)
    gw = gw.at[:GLA_LOWRANK, :QK_A].set(gate_w[0])
    gw = gw.at[GLA_LOWRANK:2 * GLA_LOWRANK, QK_A:].set(gate_w[1])
    gb = jnp.concatenate([gate_b[0], gate_b[1]])[None, :]
    return (w.astype(BF16), gw.astype(BF16), gb.astype(F32),
            q_norm_g[_PAIR_PERM][None, :], k_norm_g[_PAIR_PERM][None, :])


def _prepare(norm_g, final_norm_g, ev_w_in, ev_gla_gate_w, ev_gla_gate_b, ev_gla_norm_g,
             ev_q_norm_g, ev_k_norm_g, ev_w_out, od_w_in, od_pool_w, od_pool_b, od_pool_scale, od_w_out):
    layers = []
    for layer in range(DEPTH):
        i = layer // 2
        ng = norm_g[layer][None, :]
        if layer % 2 == 0:
            bound = (ATT_HD * (ATT_HD ** -0.5) * LOG2E * 1.02
                     * jnp.max(jnp.abs(ev_q_norm_g[i])) * jnp.max(jnp.abs(ev_k_norm_g[i])))
            layers.append((ng, _even_weights(ev_w_in[i], ev_gla_gate_w[i], ev_gla_gate_b[i],
                                             ev_q_norm_g[i], ev_k_norm_g[i]),
                           ev_gla_norm_g[i][None, :], bound, ev_w_out[i].astype(BF16)))
        else:
            fg = final_norm_g[None, :] if layer == DEPTH - 1 else None
            layers.append((ng, od_w_in[i].astype(BF16), od_pool_w[i].astype(BF16), od_pool_b[i][None, :],
                           od_pool_scale[i][None, :], od_w_out[i].astype(BF16), fg))
    return layers


def _run(x, layers, rope):
    B, S, D = x.shape
    cos2, sin2 = rope
    tm_in = min(512, S)
    tm_out = min(1024, S)
    tm_odd = min(512, S)
    gla_rows = min(1024, S)
    tq = min(1024, S)
    tk = min(512, S)
    for layer, ops in enumerate(layers):
        if layer % 2 == 0:
            ng, (w, gw, gb, qg, kg), gla_g, bound, w_out = ops
            qa, ka, va, la, sza, qb, kb, vt, szb = _ev_in(x, ng, w, gw, gb, qg, kg, cos2, sin2, tm=tm_in)
            o_bwd = _gla(qa, ka, va, la, reverse=True, rows=gla_rows)
            ma = _gla(qa, ka, va, la, reverse=False, rows=gla_rows, extra=(o_bwd, sza, gla_g))
            mb = _attention(qb, kb, vt, szb, bound, tq=tq, tk=tk)
            x = _ev_out(x, ma, mb, w_out, tm=tm_out)
        else:
            ng, win, pw, pb, ps, wout, fg = ops
            x = _odd(x, ng, win, pw, pb, ps, wout, fg, tm=tm_odd)
    return x


def _trunk(x, *weights):
    return _run(x, _prepare(*weights), _rope_tables(x.shape[1]))


def kernel(x_prompt, x_sample, norm_g, final_norm_g, ev_w_in, ev_gla_gate_w, ev_gla_gate_b,
           ev_gla_norm_g, ev_q_norm_g, ev_k_norm_g, ev_w_out, od_w_in, od_pool_w, od_pool_b,
           od_pool_scale, od_w_out):
    layers = _prepare(norm_g, final_norm_g, ev_w_in, ev_gla_gate_w, ev_gla_gate_b, ev_gla_norm_g,
                      ev_q_norm_g, ev_k_norm_g, ev_w_out, od_w_in, od_pool_w, od_pool_b, od_pool_scale,
                      od_w_out)
    rope = _rope_tables(x_prompt.shape[1])
    rope_s = rope if x_sample.shape[1] == x_prompt.shape[1] else _rope_tables(x_sample.shape[1])
    return (_run(x_prompt, layers, rope), _run(x_sample, layers, rope_s))
```

```python
import functools

import numpy as np
import jax
import jax.numpy as jnp
from jax import lax
from jax.experimental import pallas as pl
from jax.experimental.pallas import tpu as pltpu

F32 = jnp.float32
BF16 = jnp.bfloat16

D_MODEL = 1024
DEPTH = 4
GRID_W = 64
EPS = 1e-6

GLA_HEADS = 4
GLA_DK = 128
GLA_DV = 256
GLA_LOWRANK = 16
GLA_TAU = 16.0
GLA_CHUNK = 64
GLA_SUPER = 256

ATT_HEADS = 8
ATT_KV_HEADS = 2
ATT_HD = 128
ATT_GROUP = ATT_HEADS // ATT_KV_HEADS
ROPE_THETA = 10000.0
ATT_BOUND_LIMIT = 60.0
ATT_UNROLL = 8

POOL_WINDOWS = (2, 4, 8, 16)
POOL_GROUPS = 4
POOL_WIDTH = 2 * D_MODEL
POOL_GW = POOL_WIDTH // POOL_GROUPS
POOL_HALO = 8
POOL_SUB = 128
POOL_KWIN = POOL_SUB + 4 * POOL_HALO

A_WIDTH = GLA_HEADS * GLA_DV
B_WIDTH = ATT_HEADS * ATT_HD
QK_A = GLA_HEADS * GLA_DK
KV_B = ATT_KV_HEADS * ATT_HD
EV_SPLITS = (QK_A, QK_A, A_WIDTH, GLA_LOWRANK, GLA_LOWRANK, A_WIDTH, B_WIDTH, KV_B, KV_B, B_WIDTH)

C_QA = 0
C_KA = C_QA + QK_A
C_VA = C_KA + QK_A
C_ZA = C_VA + A_WIDTH
C_QB = C_ZA + A_WIDTH
C_KB = C_QB + B_WIDTH
C_VB = C_KB + KV_B
C_ZB = C_VB + KV_B
C_LR = C_ZB + B_WIDTH
LR_PAD = 128
EV_COLS = C_LR + LR_PAD

LOG2E = 1.4426950408889634
NEG_BIG = -1e30

VMEM_LIMIT = 56 * 1024 * 1024


def _rms(x, g):
    return x * lax.rsqrt(jnp.mean(x * x, axis=-1, keepdims=True) + EPS) * g


def _silu(z):
    return z * jax.nn.sigmoid(z)


def _log_sigmoid(g):
    return jnp.minimum(g, 0.0) - jnp.log(1.0 + jnp.exp(-jnp.abs(g)))


def _const_spec(shape):
    nd = len(shape)
    return pl.BlockSpec(shape, lambda *_: (0,) * nd, pipeline_mode=pl.Buffered(1))


def _ev_in_kernel(x_ref, ng_ref, w_ref, gw_ref, gb_ref, qg_ref, kg_ref, cos_ref, sin_ref,
                  qa_ref, ka_ref, va_ref, la_ref, sza_ref, qb_ref, kb_ref, vt_ref, szb_ref):
    h = _rms(x_ref[...], ng_ref[...]).astype(BF16)

    def proj(c0, width):
        return jnp.dot(h, w_ref[:, c0:c0 + width], preferred_element_type=F32)

    lr = proj(C_LR, LR_PAD).astype(BF16)
    qb = proj(C_QB, B_WIDTH)
    gate = jnp.dot(lr, gw_ref[...], preferred_element_type=F32) + gb_ref[...]
    la_ref[...] = _log_sigmoid(gate) * (1.0 / GLA_TAU)

    cos = cos_ref[...]
    sin = sin_ref[...]

    def norm_rope(xh, g):
        xn = _rms(xh, g)
        return xn * cos + pltpu.roll(xn, ATT_HD // 2, 1) * sin

    qscale = (ATT_HD ** -0.5) * LOG2E
    for hh in range(ATT_HEADS):
        sl = slice(hh * ATT_HD, (hh + 1) * ATT_HD)
        qb_ref[:, sl] = (norm_rope(qb[:, sl], qg_ref[...]) * qscale).astype(BF16)
    kb = proj(C_KB, KV_B)
    for hh in range(ATT_KV_HEADS):
        sl = slice(hh * ATT_HD, (hh + 1) * ATT_HD)
        kb_ref[:, sl] = norm_rope(kb[:, sl], kg_ref[...]).astype(BF16)

    vt_ref[...] = proj(C_VB, KV_B).T.astype(BF16)

    sza_ref[...] = _silu(proj(C_ZA, A_WIDTH)).astype(BF16)
    szb_ref[...] = _silu(proj(C_ZB, B_WIDTH)).astype(BF16)
    qa_ref[...] = (proj(C_QA, QK_A) * (GLA_DK ** -0.5)).astype(BF16)
    ka_ref[...] = proj(C_KA, QK_A).astype(BF16)
    va_ref[...] = proj(C_VA, A_WIDTH).astype(BF16)


def _ev_in(x, ng, w, gw, gb, qg, kg, cos2, sin2, *, tm):
    B, S, D = x.shape
    nt = S // tm
    row = lambda width: pl.BlockSpec((None, tm, width), lambda b, i: (b, i, 0))
    outs = [
        (QK_A, BF16), (QK_A, BF16), (A_WIDTH, BF16), (2 * QK_A, F32), (A_WIDTH, BF16),
        (B_WIDTH, BF16), (KV_B, BF16), None, (B_WIDTH, BF16),
    ]
    out_shape, out_specs = [], []
    for o in outs:
        if o is None:
            out_shape.append(jax.ShapeDtypeStruct((B, KV_B, S), BF16))
            out_specs.append(pl.BlockSpec((None, KV_B, tm), lambda b, i: (b, 0, i)))
        else:
            out_shape.append(jax.ShapeDtypeStruct((B, S, o[0]), o[1]))
            out_specs.append(row(o[0]))
    return pl.pallas_call(
        _ev_in_kernel,
        grid=(B, nt),
        in_specs=[
            row(D), _const_spec((1, D)), _const_spec((D, EV_COLS)), _const_spec((LR_PAD, 2 * QK_A)),
            _const_spec((1, 2 * QK_A)), _const_spec((1, ATT_HD)), _const_spec((1, ATT_HD)),
            pl.BlockSpec((tm, ATT_HD), lambda b, i: (i, 0)), pl.BlockSpec((tm, ATT_HD), lambda b, i: (i, 0)),
        ],
        out_specs=out_specs,
        out_shape=out_shape,
        compiler_params=pltpu.CompilerParams(
            dimension_semantics=("parallel", "parallel"), vmem_limit_bytes=VMEM_LIMIT),
        name="ev_in",
    )(x, ng, w, gw, gb, qg, kg, cos2, sin2)


def _split_hi_lo(a):
    hi = a.astype(BF16)
    lo = (a - hi.astype(F32)).astype(BF16)
    return hi, lo


def _gla_kernel(*refs, reverse, final, n_super):
    if final:
        q_ref, k_ref, v_ref, la_ref, ob_ref, sz_ref, g_ref, o_ref, st_ref = refs
    else:
        q_ref, k_ref, v_ref, la_ref, o_ref, st_ref = refs
    SC, C = GLA_SUPER, GLA_CHUNK
    n_chunk = SC // C

    @pl.when(pl.program_id(1) == 0)
    def _():
        st_ref[...] = jnp.zeros_like(st_ref)

    ri = lax.broadcasted_iota(jnp.int32, (SC, SC), 0)
    ci = lax.broadcasted_iota(jnp.int32, (SC, SC), 1)
    same = (ri // C) == (ci // C)
    if reverse:
        incl = same & (ci >= ri)
        rest = same & (ci < ri)
    else:
        incl = same & (ci <= ri)
        rest = same & (ci > ri)
    sum_ops = jnp.concatenate([incl.astype(BF16), rest.astype(BF16)], axis=0)
    lane_chunk = lax.broadcasted_iota(jnp.int32, (GLA_DK, SC), 1) // C

    order = range(n_super - 1, -1, -1) if reverse else range(n_super)
    corder = range(n_chunk - 1, -1, -1) if reverse else range(n_chunk)
    bodies = [(sc, hd) for sc in order for hd in range(GLA_HEADS)]
    rows = {sc: slice(sc * SC, (sc + 1) * SC) for sc in order}
    kcol = [slice(hd * GLA_DK, (hd + 1) * GLA_DK) for hd in range(GLA_HEADS)]
    vcol = [slice(hd * GLA_DV, (hd + 1) * GLA_DV) for hd in range(GLA_HEADS)]
    dot = functools.partial(jnp.dot, preferred_element_type=F32)

    sums = {}
    for key in bodies:
        sc, hd = key
        hi, lo = _split_hi_lo(la_ref[rows[sc], kcol[hd]])
        sums[key] = dot(sum_ops, jnp.concatenate([hi, lo], axis=1))

    q_dec, k_inv_t, k_end_t, decay = {}, {}, {}, {}
    for key in bodies:
        sc, hd = key
        sm = sums[key][:, :GLA_DK] + sums[key][:, GLA_DK:]
        b, g = sm[:SC], sm[SC:]
        k = k_ref[rows[sc], kcol[hd]].astype(F32)
        q_dec[key] = (q_ref[rows[sc], kcol[hd]].astype(F32) * jnp.exp(b)).astype(BF16)
        k_inv_t[key] = (k * jnp.exp(-b)).T.astype(BF16)
        k_end_t[key] = (k * jnp.exp(g)).T
        tot = b + g
        tot_rows = jnp.concatenate([tot[c * C:c * C + 8] for c in range(n_chunk)]
                                   + [jnp.zeros((SC - 8 * n_chunk, GLA_DK), F32)], axis=0)
        decay[key] = jnp.exp(tot_rows.T)

    scores = {key: dot(q_dec[key], k_inv_t[key]) for key in bodies}
    o_sc = {}
    for key in bodies:
        sc, hd = key
        o_sc[key] = dot(jnp.where(incl, scores[key], 0.0).astype(BF16), v_ref[rows[sc], vcol[hd]])
    kv = {}
    for key in bodies:
        sc, hd = key
        for c in corder:
            k_end_c = jnp.where(lane_chunk == c, k_end_t[key], 0.0).astype(BF16)
            kv[key, c] = dot(k_end_c, v_ref[rows[sc], vcol[hd]])

    inter = {}
    for sc in order:
        for c in corder:
            for hd in range(GLA_HEADS):
                key = (sc, hd)
                state = st_ref[hd]
                inter[key, c] = dot(q_dec[key][c * C:(c + 1) * C, :], state.astype(BF16))
                st_ref[hd] = state * decay[key][:, 8 * c:8 * c + 1] + kv[key, c]

    for key in bodies:
        sc, hd = key
        o = o_sc[key] + jnp.concatenate([inter[key, c] for c in range(n_chunk)], axis=0)
        if final:
            o = o + ob_ref[rows[sc], vcol[hd]]
            o = _rms(o, g_ref[:, vcol[hd]]) * sz_ref[rows[sc], vcol[hd]].astype(F32)
        o_ref[rows[sc], vcol[hd]] = o.astype(o_ref.dtype)


def _gla(q, k, v, la, *, reverse, rows, extra=None):
    B, S, _ = q.shape
    nb = S // rows
    final = extra is not None
    blk = (lambda i: nb - 1 - i) if reverse else (lambda i: i)
    la_blk = 1 if reverse else 0
    qk_spec = pl.BlockSpec((None, rows, QK_A), lambda b, i: (b, blk(i), 0))
    v_spec = pl.BlockSpec((None, rows, A_WIDTH), lambda b, i: (b, blk(i), 0))
    la_spec = pl.BlockSpec((None, rows, QK_A), lambda b, i: (b, blk(i), la_blk))
    in_specs = [qk_spec, qk_spec, v_spec, la_spec]
    args = [q, k, v, la]
    if final:
        ob, sz, g = extra
        in_specs += [v_spec, v_spec, _const_spec((1, A_WIDTH))]
        args += [ob, sz, g]
    return pl.pallas_call(
        functools.partial(_gla_kernel, reverse=reverse, final=final, n_super=rows // GLA_SUPER),
        grid=(B, nb),
        in_specs=in_specs,
        out_specs=v_spec,
        out_shape=jax.ShapeDtypeStruct((B, S, A_WIDTH), BF16 if final else F32),
        scratch_shapes=[pltpu.VMEM((GLA_HEADS, GLA_DK, GLA_DV), F32)],
        compiler_params=pltpu.CompilerParams(
            dimension_semantics=("parallel", "arbitrary"), vmem_limit_bytes=VMEM_LIMIT),
        name="gla_fwd" if final else "gla_bwd",
    )(*args)


def _load_qt(q_ref, qt_ref, tq):
    for hh in range(ATT_GROUP):
        qt_ref[:, hh * tq:(hh + 1) * tq] = q_ref[:, hh * ATT_HD:(hh + 1) * ATT_HD].astype(F32).T.astype(BF16)


def _attn_finish(acc_ref, l_ref, sz_ref, o_ref, tq):
    o_t = acc_ref[...] * (1.0 / jnp.sum(l_ref[...], axis=0, keepdims=True))
    for hh in range(ATT_GROUP):
        sl = slice(hh * ATT_HD, (hh + 1) * ATT_HD)
        o_ref[:, sl] = (o_t[:, hh * tq:(hh + 1) * tq].T * sz_ref[:, sl].astype(F32)).astype(o_ref.dtype)


def _sublane_partial_sums(p):
    return p.reshape(p.shape[0] // 8, 8, p.shape[1]).sum(axis=0)


def _attn_bounded_kernel(q_ref, k_ref, vt_ref, sz_ref, o_ref, qt_ref, p_ref, acc_ref, l_ref, *, tk):
    S = k_ref.shape[0]
    tq = q_ref.shape[0]
    nk = S // tk
    _load_qt(q_ref, qt_ref, tq)
    acc_ref[...] = jnp.zeros_like(acc_ref)
    l_ref[...] = jnp.zeros_like(l_ref)
    p_ref[1] = jnp.zeros_like(p_ref[1])

    def probs(j, slot):
        off = pl.multiple_of(j * tk, tk)
        s = jnp.dot(k_ref[pl.ds(off, tk), :], qt_ref[...], preferred_element_type=F32)
        p = jnp.exp2(s)
        l_ref[...] += _sublane_partial_sums(p)
        p_ref[slot] = p.astype(BF16)

    def accumulate(j, slot):
        off = pl.multiple_of(j * tk, tk)
        acc_ref[...] += jnp.dot(vt_ref[:, pl.ds(off, tk)], p_ref[slot], preferred_element_type=F32)

    unroll = min(ATT_UNROLL, nk)

    def group(jj, carry):
        j = unroll * jj
        for u in range(unroll):
            probs(j + u, u % 2)
            accumulate(jnp.maximum(j + u - 1, 0), (u + 1) % 2)
        return carry

    lax.fori_loop(0, nk // unroll, group, 0)
    accumulate(nk - 1, (nk - 1) % 2)
    _attn_finish(acc_ref, l_ref, sz_ref, o_ref, tq)


def _attn_online_kernel(q_ref, k_ref, vt_ref, sz_ref, o_ref, qt_ref, s_ref, mx_ref, acc_ref, l_ref, m_ref, *, tk):
    S = k_ref.shape[0]
    tq = q_ref.shape[0]
    nk = S // tk
    _load_qt(q_ref, qt_ref, tq)
    acc_ref[...] = jnp.zeros_like(acc_ref)
    l_ref[...] = jnp.zeros_like(l_ref)
    m_ref[...] = jnp.full_like(m_ref, NEG_BIG)

    def scores(j, slot):
        off = pl.multiple_of(j * tk, tk)
        s = jnp.dot(k_ref[pl.ds(off, tk), :], qt_ref[...], preferred_element_type=F32)
        s_ref[slot] = s
        mx_ref[slot] = jnp.max(s, axis=0, keepdims=True)

    def accumulate(j, slot):
        off = pl.multiple_of(j * tk, tk)
        m_old = m_ref[...]
        m_new = jnp.maximum(m_old, mx_ref[slot])
        alpha = jnp.exp2(m_old - m_new)
        p = jnp.exp2(s_ref[slot] - m_new)
        l_ref[...] = l_ref[...] * alpha + _sublane_partial_sums(p)
        pv = jnp.dot(vt_ref[:, pl.ds(off, tk)], p.astype(BF16), preferred_element_type=F32)
        acc_ref[...] = acc_ref[...] * alpha + pv
        m_ref[...] = m_new

    scores(0, 0)

    def pair(jj, carry):
        j = 2 * jj
        scores(j + 1, 1)
        accumulate(j, 0)
        scores(jnp.minimum(j + 2, nk - 1), 0)
        accumulate(j + 1, 1)
        return carry

    lax.fori_loop(0, nk // 2, pair, 0)
    _attn_finish(acc_ref, l_ref, sz_ref, o_ref, tq)


def _attention(qb, kb, vt, szb, score_bound, *, tq, tk):
    B, S, _ = qb.shape
    nk = S // tk
    assert S % tq == 0 and nk % 2 == 0 and nk % min(ATT_UNROLL, nk) == 0
    gw = ATT_GROUP * ATT_HD
    n = ATT_GROUP * tq

    def call(body, scratch, name):
        q_spec = pl.BlockSpec((None, tq, gw), lambda b, kh, i: (b, i, kh))
        return pl.pallas_call(
            functools.partial(body, tk=tk),
            grid=(B, ATT_KV_HEADS, S // tq),
            in_specs=[
                q_spec,
                pl.BlockSpec((None, S, ATT_HD), lambda b, kh, i: (b, 0, kh)),
                pl.BlockSpec((None, ATT_HD, S), lambda b, kh, i: (b, kh, 0)),
                q_spec,
            ],
            out_specs=q_spec,
            out_shape=jax.ShapeDtypeStruct((B, S, B_WIDTH), BF16),
            scratch_shapes=scratch,
            compiler_params=pltpu.CompilerParams(
                dimension_semantics=("parallel", "parallel", "arbitrary"), vmem_limit_bytes=VMEM_LIMIT),
            name=name)

    bounded = call(_attn_bounded_kernel, [
        pltpu.VMEM((ATT_HD, n), BF16), pltpu.VMEM((2, tk, n), BF16),
        pltpu.VMEM((ATT_HD, n), F32), pltpu.VMEM((8, n), F32)], "attn_bounded")
    online = call(_attn_online_kernel, [
        pltpu.VMEM((ATT_HD, n), BF16), pltpu.VMEM((2, tk, n), F32), pltpu.VMEM((2, 1, n), F32),
        pltpu.VMEM((ATT_HD, n), F32), pltpu.VMEM((8, n), F32), pltpu.VMEM((1, n), F32)], "attn_online")
    return lax.cond(score_bound < ATT_BOUND_LIMIT, bounded, online, qb, kb, vt, szb)


def _ev_out_kernel(x_ref, ma_ref, mb_ref, w_ref, o_ref):
    acc = jnp.dot(ma_ref[...], w_ref[:A_WIDTH, :], preferred_element_type=F32)
    acc = acc + jnp.dot(mb_ref[...], w_ref[A_WIDTH:, :], preferred_element_type=F32)
    o_ref[...] = x_ref[...] + acc


def _ev_out(x, ma, mb, w, *, tm):
    B, S, D = x.shape
    row = lambda width: pl.BlockSpec((None, tm, width), lambda b, i: (b, i, 0))
    return pl.pallas_call(
        _ev_out_kernel,
        grid=(B, S // tm),
        in_specs=[row(D), row(A_WIDTH), row(B_WIDTH), _const_spec((A_WIDTH + B_WIDTH, D))],
        out_specs=row(D),
        out_shape=jax.ShapeDtypeStruct((B, S, D), F32),
        compiler_params=pltpu.CompilerParams(
            dimension_semantics=("parallel", "parallel"), vmem_limit_bytes=VMEM_LIMIT),
        name="ev_out",
    )(x, ma, mb, w)


def _odd_kernel(*refs, seq_len, final):
    if final:
        xp_ref, x_ref, xn_ref, ng_ref, win_ref, pw_ref, pb_ref, ps_ref, wout_ref, fg_ref, o_ref = refs
    else:
        xp_ref, x_ref, xn_ref, ng_ref, win_ref, pw_ref, pb_ref, ps_ref, wout_ref, o_ref = refs
    tm = x_ref.shape[0]
    H = POOL_HALO
    t0 = pl.program_id(1) * tm
    x = x_ref[...]
    xe = jnp.concatenate([x, xp_ref[...], xn_ref[...]], axis=0)
    h = _rms(xe, ng_ref[...]).astype(BF16)
    ue = jnp.dot(h, win_ref[:, :POOL_WIDTH], preferred_element_type=F32)
    z = jnp.dot(h[:tm], win_ref[:, POOL_WIDTH:], preferred_element_type=F32)
    ue_bf = ue.astype(BF16)

    n_sub = tm // POOL_SUB
    pooled = [[None] * n_sub for _ in range(POOL_GROUPS)]
    inv_cnt = [[None] * n_sub for _ in range(POOL_GROUPS)]
    c = lax.broadcasted_iota(jnp.int32, (1, POOL_KWIN + 2 * H), 1)
    for sb in range(n_sub):
        k0 = min(max(sb * POOL_SUB - 2 * H, 0), tm - POOL_KWIN)
        t = t0 + sb * POOL_SUB + lax.broadcasted_iota(jnp.int32, (POOL_SUB, 1), 0)
        pos = t0 + jnp.where(c < POOL_KWIN, k0 + c,
                             jnp.where(c < POOL_KWIN + H, c - POOL_KWIN - H, c - POOL_KWIN - H + tm))
        valid = (pos >= 0) & (pos < seq_len)
        keys = jnp.concatenate([ue_bf[k0:k0 + POOL_KWIN], ue_bf[tm:]], axis=0)
        for g, w in enumerate(POOL_WINDOWS):
            lo = t - w // 2
            hi = lo + w
            band = ((pos >= lo) & (pos < hi) & valid).astype(BF16)
            inv_cnt[g][sb] = 1.0 / (jnp.minimum(hi, seq_len) - jnp.maximum(lo, 0)).astype(F32)
            pooled[g][sb] = jnp.dot(band, keys[:, g * POOL_GW:(g + 1) * POOL_GW], preferred_element_type=F32)
    pooled = [jnp.concatenate(pg, axis=0) for pg in pooled]
    inv_cnt = [jnp.concatenate(ig, axis=0) for ig in inv_cnt]
    ms = []
    for g in range(POOL_GROUPS):
        d = (pooled[g] * inv_cnt[g] - ue[:tm, g * POOL_GW:(g + 1) * POOL_GW]).astype(BF16)
        ms.append(jnp.dot(d, pw_ref[g], preferred_element_type=F32))
    m = jnp.concatenate(ms, axis=1) + pb_ref[...]
    m = (m * ps_ref[...] * _silu(z)).astype(BF16)
    y = x + jnp.dot(m, wout_ref[...], preferred_element_type=F32)
    if final:
        y = _rms(y, fg_ref[...])
    o_ref[...] = y


def _odd(x, ng, win, pw, pb, ps, wout, fg, *, tm):
    B, S, D = x.shape
    H = POOL_HALO
    nh = tm // H
    last = S // H - 1
    row = pl.BlockSpec((None, tm, D), lambda b, i: (b, i, 0))
    prev = pl.BlockSpec((None, H, D), lambda b, i: (b, jnp.maximum(i * nh - 1, 0), 0))
    nxt = pl.BlockSpec((None, H, D), lambda b, i: (b, jnp.minimum((i + 1) * nh, last), 0))
    final = fg is not None
    in_specs = [
        prev, row, nxt, _const_spec((1, D)), _const_spec((D, 2 * POOL_WIDTH)),
        _const_spec((POOL_GROUPS, POOL_GW, POOL_GW)), _const_spec((1, POOL_WIDTH)),
        _const_spec((1, POOL_WIDTH)), _const_spec((POOL_WIDTH, D)),
    ]
    args = [x, x, x, ng, win, pw, pb, ps, wout]
    if final:
        in_specs.append(_const_spec((1, D)))
        args.append(fg)
    return pl.pallas_call(
        functools.partial(_odd_kernel, seq_len=S, final=final),
        grid=(B, S // tm),
        in_specs=in_specs,
        out_specs=row,
        out_shape=jax.ShapeDtypeStruct((B, S, D), F32),
        compiler_params=pltpu.CompilerParams(
            dimension_semantics=("parallel", "parallel"), vmem_limit_bytes=VMEM_LIMIT),
        name="odd_final" if final else "odd",
    )(*args)


def _rope_tables(S):
    rows = S // GRID_W
    row = jnp.repeat(jnp.arange(rows, dtype=F32), GRID_W)
    col = jnp.tile(jnp.arange(GRID_W, dtype=F32), rows)
    n_pairs = ATT_HD // 4
    freqs = ROPE_THETA ** (-jnp.arange(n_pairs, dtype=F32) / n_pairs)
    ang = jnp.concatenate([row[:, None] * freqs, col[:, None] * freqs], axis=-1)
    c, s = jnp.cos(ang), jnp.sin(ang)
    return jnp.concatenate([c, c], axis=-1), jnp.concatenate([-s, s], axis=-1)


_PAIR_PERM = np.concatenate([np.arange(0, ATT_HD, 2), np.arange(1, ATT_HD, 2)])


def _even_weights(w_in, gate_w, gate_b, q_norm_g, k_norm_g):
    offs = np.concatenate([[0], np.cumsum(EV_SPLITS)])
    w_in = w_in.astype(BF16)
    seg = lambda i: w_in[:, offs[i]:offs[i + 1]]
    q_a, k_a, v_a, lr_f, lr_b, z_a, q_b, k_b, v_b, z_b = [seg(i) for i in range(10)]
    D = w_in.shape[0]

    def perm_heads(w, nh):
        return w.reshape(D, nh, ATT_HD)[:, :, _PAIR_PERM].reshape(D, nh * ATT_HD)

    lr = jnp.concatenate([lr_f, lr_b, jnp.zeros((D, LR_PAD - 2 * GLA_LOWRANK), w_in.dtype)], axis=1)
    w = jnp.concatenate(
        [q_a, k_a, v_a, z_a, perm_heads(q_b, ATT_HEADS), perm_heads(k_b, ATT_KV_HEADS), v_b, z_b, lr], axis=1)
    gw = jnp.zeros((LR_PAD, 2 * QK_A), F32)
    gw = gw.at[:GLA_LOWRANK, :QK_A].set(gate_w[0])
    gw = gw.at[GLA_LOWRANK:2 * GLA_LOWRANK, QK_A:].set(gate_w[1])
    gb = jnp.concatenate([gate_b[0], gate_b[1]])[None, :]
    return (w, gw.astype(BF16), gb.astype(F32),
            q_norm_g[_PAIR_PERM][None, :], k_norm_g[_PAIR_PERM][None, :])


def _prepare(norm_g, final_norm_g, ev_w_in, ev_gla_gate_w, ev_gla_gate_b, ev_gla_norm_g,
             ev_q_norm_g, ev_k_norm_g, ev_w_out, od_w_in, od_pool_w, od_pool_b, od_pool_scale, od_w_out):
    layers = []
    for layer in range(DEPTH):
        i = layer // 2
        ng = norm_g[layer][None, :]
        if layer % 2 == 0:
            bound = (ATT_HD * (ATT_HD ** -0.5) * LOG2E * 1.02
                     * jnp.max(jnp.abs(ev_q_norm_g[i])) * jnp.max(jnp.abs(ev_k_norm_g[i])))
            layers.append((ng, _even_weights(ev_w_in[i], ev_gla_gate_w[i], ev_gla_gate_b[i],
                                             ev_q_norm_g[i], ev_k_norm_g[i]),
                           ev_gla_norm_g[i][None, :], bound, ev_w_out[i].astype(BF16)))
        else:
            fg = final_norm_g[None, :] if layer == DEPTH - 1 else None
            layers.append((ng, od_w_in[i].astype(BF16), od_pool_w[i].astype(BF16), od_pool_b[i][None, :],
                           od_pool_scale[i][None, :], od_w_out[i].astype(BF16), fg))
    return layers


def _run(x, layers, rope):
    B, S, D = x.shape
    cos2, sin2 = rope
    tm_in = min(512, S)
    tm_out = min(1024, S)
    tm_odd = min(512, S)
    gla_rows = min(1024, S)
    tq = min(1024, S)
    tk = min(512, S)
    for layer, ops in enumerate(layers):
        if layer % 2 == 0:
            ng, (w, gw, gb, qg, kg), gla_g, bound, w_out = ops
            qa, ka, va, la, sza, qb, kb, vt, szb = _ev_in(x, ng, w, gw, gb, qg, kg, cos2, sin2, tm=tm_in)
            o_bwd = _gla(qa, ka, va, la, reverse=True, rows=gla_rows)
            ma = _gla(qa, ka, va, la, reverse=False, rows=gla_rows, extra=(o_bwd, sza, gla_g))
            mb = _attention(qb, kb, vt, szb, bound, tq=tq, tk=tk)
            x = _ev_out(x, ma, mb, w_out, tm=tm_out)
        else:
            ng, win, pw, pb, ps, wout, fg = ops
            x = _odd(x, ng, win, pw, pb, ps, wout, fg, tm=tm_odd)
    return x


def _trunk(x, *weights):
    return _run(x, _prepare(*weights), _rope_tables(x.shape[1]))


def kernel(x_prompt, x_sample, norm_g, final_norm_g, ev_w_in, ev_gla_gate_w, ev_gla_gate_b,
           ev_gla_norm_g, ev_q_norm_g, ev_k_norm_g, ev_w_out, od_w_in, od_pool_w, od_pool_b,
           od_pool_scale, od_w_out):
    layers = _prepare(norm_g, final_norm_g, ev_w_in, ev_gla_gate_w, ev_gla_gate_b, ev_gla_norm_g,
                      ev_q_norm_g, ev_k_norm_g, ev_w_out, od_w_in, od_pool_w, od_pool_b, od_pool_scale,
                      od_w_out)
    rope = _rope_tables(x_prompt.shape[1])
    rope_s = rope if x_sample.shape[1] == x_prompt.shape[1] else _rope_tables(x_sample.shape[1])
    return (_run(x_prompt, layers, rope), _run(x_sample, layers, rope_s))
```

```python
import functools

import numpy as np
import jax
import jax.numpy as jnp
from jax import lax
from jax.experimental import pallas as pl
from jax.experimental.pallas import tpu as pltpu

F32 = jnp.float32
BF16 = jnp.bfloat16

D_MODEL = 1024
DEPTH = 4
GRID_W = 64
EPS = 1e-6

GLA_HEADS = 4
GLA_DK = 128
GLA_DV = 256
GLA_LOWRANK = 16
GLA_TAU = 16.0
GLA_CHUNK = 64
GLA_SUPER = 256

ATT_HEADS = 8
ATT_KV_HEADS = 2
ATT_HD = 128
ATT_GROUP = ATT_HEADS // ATT_KV_HEADS
ROPE_THETA = 10000.0
ATT_BOUND_LIMIT = 60.0
ATT_UNROLL = 8

POOL_WINDOWS = (2, 4, 8, 16)
POOL_GROUPS = 4
POOL_WIDTH = 2 * D_MODEL
POOL_GW = POOL_WIDTH // POOL_GROUPS
POOL_HALO = 8
POOL_SUB = 128
POOL_KWIN = POOL_SUB + 4 * POOL_HALO

A_WIDTH = GLA_HEADS * GLA_DV
B_WIDTH = ATT_HEADS * ATT_HD
QK_A = GLA_HEADS * GLA_DK
KV_B = ATT_KV_HEADS * ATT_HD
EV_SPLITS = (QK_A, QK_A, A_WIDTH, GLA_LOWRANK, GLA_LOWRANK, A_WIDTH, B_WIDTH, KV_B, KV_B, B_WIDTH)

C_QA = 0
C_KA = C_QA + QK_A
C_VA = C_KA + QK_A
C_ZA = C_VA + A_WIDTH
C_QB = C_ZA + A_WIDTH
C_KB = C_QB + B_WIDTH
C_VB = C_KB + KV_B
C_ZB = C_VB + KV_B
C_LR = C_ZB + B_WIDTH
LR_PAD = 128
EV_COLS = C_LR + LR_PAD

LOG2E = 1.4426950408889634
NEG_BIG = -1e30

VMEM_LIMIT = 56 * 1024 * 1024


def _rms(x, g):
    return x * lax.rsqrt(jnp.mean(x * x, axis=-1, keepdims=True) + EPS) * g


def _silu(z):
    return z * jax.nn.sigmoid(z)


def _log_sigmoid(g):
    return jnp.minimum(g, 0.0) - jnp.log(1.0 + jnp.exp(-jnp.abs(g)))


def _const_spec(shape):
    nd = len(shape)
    return pl.BlockSpec(shape, lambda *_: (0,) * nd, pipeline_mode=pl.Buffered(1))


def _ev_in_kernel(x_ref, ng_ref, w_ref, gw_ref, gb_ref, qg_ref, kg_ref, cos_ref, sin_ref,
                  qa_ref, ka_ref, va_ref, la_ref, sza_ref, qb_ref, kb_ref, vt_ref, szb_ref):
    h = _rms(x_ref[...], ng_ref[...]).astype(BF16)

    def proj(c0, width):
        return jnp.dot(h, w_ref[:, c0:c0 + width], preferred_element_type=F32)

    lr = proj(C_LR, LR_PAD).astype(BF16)
    qb = proj(C_QB, B_WIDTH)
    gate = jnp.dot(lr, gw_ref[...], preferred_element_type=F32) + gb_ref[...]
    la_ref[...] = _log_sigmoid(gate) * (1.0 / GLA_TAU)

    cos = cos_ref[...]
    sin = sin_ref[...]

    def norm_rope(xh, g):
        xn = _rms(xh, g)
        return xn * cos + pltpu.roll(xn, ATT_HD // 2, 1) * sin

    qscale = (ATT_HD ** -0.5) * LOG2E
    for hh in range(ATT_HEADS):
        sl = slice(hh * ATT_HD, (hh + 1) * ATT_HD)
        qb_ref[:, sl] = (norm_rope(qb[:, sl], qg_ref[...]) * qscale).astype(BF16)
    kb = proj(C_KB, KV_B)
    for hh in range(ATT_KV_HEADS):
        sl = slice(hh * ATT_HD, (hh + 1) * ATT_HD)
        kb_ref[:, sl] = norm_rope(kb[:, sl], kg_ref[...]).astype(BF16)

    vt_ref[...] = proj(C_VB, KV_B).T.astype(BF16)

    sza_ref[...] = _silu(proj(C_ZA, A_WIDTH)).astype(BF16)
    szb_ref[...] = _silu(proj(C_ZB, B_WIDTH)).astype(BF16)
    qa_ref[...] = (proj(C_QA, QK_A) * (GLA_DK ** -0.5)).astype(BF16)
    ka_ref[...] = proj(C_KA, QK_A).astype(BF16)
    va_ref[...] = proj(C_VA, A_WIDTH).astype(BF16)


def _ev_in(x, ng, w, gw, gb, qg, kg, cos2, sin2, *, tm):
    B, S, D = x.shape
    nt = S // tm
    row = lambda width: pl.BlockSpec((None, tm, width), lambda b, i: (b, i, 0))
    outs = [
        (QK_A, BF16), (QK_A, BF16), (A_WIDTH, BF16), (2 * QK_A, F32), (A_WIDTH, BF16),
        (B_WIDTH, BF16), (KV_B, BF16), None, (B_WIDTH, BF16),
    ]
    out_shape, out_specs = [], []
    for o in outs:
        if o is None:
            out_shape.append(jax.ShapeDtypeStruct((B, KV_B, S), BF16))
            out_specs.append(pl.BlockSpec((None, KV_B, tm), lambda b, i: (b, 0, i)))
        else:
            out_shape.append(jax.ShapeDtypeStruct((B, S, o[0]), o[1]))
            out_specs.append(row(o[0]))
    return pl.pallas_call(
        _ev_in_kernel,
        grid=(B, nt),
        in_specs=[
            row(D), _const_spec((1, D)), _const_spec((D, EV_COLS)), _const_spec((LR_PAD, 2 * QK_A)),
            _const_spec((1, 2 * QK_A)), _const_spec((1, ATT_HD)), _const_spec((1, ATT_HD)),
            pl.BlockSpec((tm, ATT_HD), lambda b, i: (i, 0)), pl.BlockSpec((tm, ATT_HD), lambda b, i: (i, 0)),
        ],
        out_specs=out_specs,
        out_shape=out_shape,
        compiler_params=pltpu.CompilerParams(
            dimension_semantics=("parallel", "parallel"), vmem_limit_bytes=VMEM_LIMIT),
        name="ev_in",
    )(x, ng, w, gw, gb, qg, kg, cos2, sin2)


def _split_hi_lo(a):
    hi = a.astype(BF16)
    lo = (a - hi.astype(F32)).astype(BF16)
    return hi, lo


def _gla_kernel(*refs, reverse, final, n_super):
    if final:
        q_ref, k_ref, v_ref, la_ref, ob_ref, sz_ref, g_ref, o_ref, st_ref = refs
    else:
        q_ref, k_ref, v_ref, la_ref, o_ref, st_ref = refs
    SC, C = GLA_SUPER, GLA_CHUNK
    n_chunk = SC // C

    @pl.when(pl.program_id(1) == 0)
    def _():
        st_ref[...] = jnp.zeros_like(st_ref)

    ri = lax.broadcasted_iota(jnp.int32, (SC, SC), 0)
    ci = lax.broadcasted_iota(jnp.int32, (SC, SC), 1)
    same = (ri // C) == (ci // C)
    if reverse:
        incl = same & (ci >= ri)
        rest = same & (ci < ri)
    else:
        incl = same & (ci <= ri)
        rest = same & (ci > ri)
    sum_ops = jnp.concatenate([incl.astype(BF16), rest.astype(BF16)], axis=0)
    lane_chunk = lax.broadcasted_iota(jnp.int32, (GLA_DK, SC), 1) // C

    order = range(n_super - 1, -1, -1) if reverse else range(n_super)
    corder = range(n_chunk - 1, -1, -1) if reverse else range(n_chunk)
    bodies = [(sc, hd) for sc in order for hd in range(GLA_HEADS)]
    rows = {sc: slice(sc * SC, (sc + 1) * SC) for sc in order}
    kcol = [slice(hd * GLA_DK, (hd + 1) * GLA_DK) for hd in range(GLA_HEADS)]
    vcol = [slice(hd * GLA_DV, (hd + 1) * GLA_DV) for hd in range(GLA_HEADS)]
    dot = functools.partial(jnp.dot, preferred_element_type=F32)

    sums = {}
    for key in bodies:
        sc, hd = key
        hi, lo = _split_hi_lo(la_ref[rows[sc], kcol[hd]])
        sums[key] = dot(sum_ops, jnp.concatenate([hi, lo], axis=1))

    q_dec, k_inv_t, k_end_t, decay = {}, {}, {}, {}
    for key in bodies:
        sc, hd = key
        sm = sums[key][:, :GLA_DK] + sums[key][:, GLA_DK:]
        b, g = sm[:SC], sm[SC:]
        k = k_ref[rows[sc], kcol[hd]].astype(F32)
        q_dec[key] = (q_ref[rows[sc], kcol[hd]].astype(F32) * jnp.exp(b)).astype(BF16)
        k_inv_t[key] = (k * jnp.exp(-b)).T.astype(BF16)
        k_end_t[key] = (k * jnp.exp(g)).T
        tot = b + g
        tot_rows = jnp.concatenate([tot[c * C:c * C + 8] for c in range(n_chunk)]
                                   + [jnp.zeros((SC - 8 * n_chunk, GLA_DK), F32)], axis=0)
        decay[key] = jnp.exp(tot_rows.T)

    scores = {key: dot(q_dec[key], k_inv_t[key]) for key in bodies}
    o_sc = {}
    for key in bodies:
        sc, hd = key
        o_sc[key] = dot(jnp.where(incl, scores[key], 0.0).astype(BF16), v_ref[rows[sc], vcol[hd]])
    kv = {}
    for key in bodies:
        sc, hd = key
        for c in corder:
            k_end_c = jnp.where(lane_chunk == c, k_end_t[key], 0.0).astype(BF16)
            kv[key, c] = dot(k_end_c, v_ref[rows[sc], vcol[hd]])

    inter = {}
    for sc in order:
        for c in corder:
            for hd in range(GLA_HEADS):
                key = (sc, hd)
                state = st_ref[hd]
                inter[key, c] = dot(q_dec[key][c * C:(c + 1) * C, :], state.astype(BF16))
                st_ref[hd] = state * decay[key][:, 8 * c:8 * c + 1] + kv[key, c]

    for key in bodies:
        sc, hd = key
        o = o_sc[key] + jnp.concatenate([inter[key, c] for c in range(n_chunk)], axis=0)
        if final:
            o = o + ob_ref[rows[sc], vcol[hd]]
            o = _rms(o, g_ref[:, vcol[hd]]) * sz_ref[rows[sc], vcol[hd]].astype(F32)
        o_ref[rows[sc], vcol[hd]] = o.astype(o_ref.dtype)


def _gla(q, k, v, la, *, reverse, rows, extra=None):
    B, S, _ = q.shape
    nb = S // rows
    final = extra is not None
    blk = (lambda i: nb - 1 - i) if reverse else (lambda i: i)
    la_blk = 1 if reverse else 0
    qk_spec = pl.BlockSpec((None, rows, QK_A), lambda b, i: (b, blk(i), 0))
    v_spec = pl.BlockSpec((None, rows, A_WIDTH), lambda b, i: (b, blk(i), 0))
    la_spec = pl.BlockSpec((None, rows, QK_A), lambda b, i: (b, blk(i), la_blk))
    in_specs = [qk_spec, qk_spec, v_spec, la_spec]
    args = [q, k, v, la]
    if final:
        ob, sz, g = extra
        in_specs += [v_spec, v_spec, _const_spec((1, A_WIDTH))]
        args += [ob, sz, g]
    return pl.pallas_call(
        functools.partial(_gla_kernel, reverse=reverse, final=final, n_super=rows // GLA_SUPER),
        grid=(B, nb),
        in_specs=in_specs,
        out_specs=v_spec,
        out_shape=jax.ShapeDtypeStruct((B, S, A_WIDTH), BF16 if final else F32),
        scratch_shapes=[pltpu.VMEM((GLA_HEADS, GLA_DK, GLA_DV), F32)],
        compiler_params=pltpu.CompilerParams(
            dimension_semantics=("parallel", "arbitrary"), vmem_limit_bytes=VMEM_LIMIT),
        name="gla_fwd" if final else "gla_bwd",
    )(*args)


def _load_qt(q_ref, qt_ref, tq):
    for hh in range(ATT_GROUP):
        qt_ref[:, hh * tq:(hh + 1) * tq] = q_ref[:, hh * ATT_HD:(hh + 1) * ATT_HD].astype(F32).T.astype(BF16)


def _attn_finish(acc_ref, l_ref, sz_ref, o_ref, tq):
    o_t = acc_ref[...] * (1.0 / jnp.sum(l_ref[...], axis=0, keepdims=True))
    for hh in range(ATT_GROUP):
        sl = slice(hh * ATT_HD, (hh + 1) * ATT_HD)
        o_ref[:, sl] = (o_t[:, hh * tq:(hh + 1) * tq].T * sz_ref[:, sl].astype(F32)).astype(o_ref.dtype)


def _sublane_partial_sums(p):
    return p.reshape(p.shape[0] // 8, 8, p.shape[1]).sum(axis=0)


def _attn_bounded_kernel(q_ref, k_ref, vt_ref, sz_ref, o_ref, qt_ref, p_ref, acc_ref, l_ref, *, tk):
    S = k_ref.shape[0]
    tq = q_ref.shape[0]
    nk = S // tk
    _load_qt(q_ref, qt_ref, tq)
    acc_ref[...] = jnp.zeros_like(acc_ref)
    l_ref[...] = jnp.zeros_like(l_ref)
    p_ref[1] = jnp.zeros_like(p_ref[1])

    def probs(j, slot):
        off = pl.multiple_of(j * tk, tk)
        s = jnp.dot(k_ref[pl.ds(off, tk), :], qt_ref[...], preferred_element_type=F32)
        p = jnp.exp2(s)
        l_ref[...] += _sublane_partial_sums(p)
        p_ref[slot] = p.astype(BF16)

    def accumulate(j, slot):
        off = pl.multiple_of(j * tk, tk)
        acc_ref[...] += jnp.dot(vt_ref[:, pl.ds(off, tk)], p_ref[slot], preferred_element_type=F32)

    unroll = min(ATT_UNROLL, nk)

    def group(jj, carry):
        j = unroll * jj
        for u in range(unroll):
            probs(j + u, u % 2)
            accumulate(jnp.maximum(j + u - 1, 0), (u + 1) % 2)
        return carry

    lax.fori_loop(0, nk // unroll, group, 0)
    accumulate(nk - 1, (nk - 1) % 2)
    _attn_finish(acc_ref, l_ref, sz_ref, o_ref, tq)


def _attn_online_kernel(q_ref, k_ref, vt_ref, sz_ref, o_ref, qt_ref, s_ref, mx_ref, acc_ref, l_ref, m_ref, *, tk):
    S = k_ref.shape[0]
    tq = q_ref.shape[0]
    nk = S // tk
    _load_qt(q_ref, qt_ref, tq)
    acc_ref[...] = jnp.zeros_like(acc_ref)
    l_ref[...] = jnp.zeros_like(l_ref)
    m_ref[...] = jnp.full_like(m_ref, NEG_BIG)

    def scores(j, slot):
        off = pl.multiple_of(j * tk, tk)
        s = jnp.dot(k_ref[pl.ds(off, tk), :], qt_ref[...], preferred_element_type=F32)
        s_ref[slot] = s
        mx_ref[slot] = jnp.max(s, axis=0, keepdims=True)

    def accumulate(j, slot):
        off = pl.multiple_of(j * tk, tk)
        m_old = m_ref[...]
        m_new = jnp.maximum(m_old, mx_ref[slot])
        alpha = jnp.exp2(m_old - m_new)
        p = jnp.exp2(s_ref[slot] - m_new)
        l_ref[...] = l_ref[...] * alpha + _sublane_partial_sums(p)
        pv = jnp.dot(vt_ref[:, pl.ds(off, tk)], p.astype(BF16), preferred_element_type=F32)
        acc_ref[...] = acc_ref[...] * alpha + pv
        m_ref[...] = m_new

    scores(0, 0)

    def pair(jj, carry):
        j = 2 * jj
        scores(j + 1, 1)
        accumulate(j, 0)
        scores(jnp.minimum(j + 2, nk - 1), 0)
        accumulate(j + 1, 1)
        return carry

    lax.fori_loop(0, nk // 2, pair, 0)
    _attn_finish(acc_ref, l_ref, sz_ref, o_ref, tq)


def _attention(qb, kb, vt, szb, score_bound, *, tq, tk):
    B, S, _ = qb.shape
    nk = S // tk
    assert S % tq == 0 and nk % 2 == 0 and nk % min(ATT_UNROLL, nk) == 0
    gw = ATT_GROUP * ATT_HD
    n = ATT_GROUP * tq

    def call(body, scratch, name):
        q_spec = pl.BlockSpec((None, tq, gw), lambda b, kh, i: (b, i, kh))
        return pl.pallas_call(
            functools.partial(body, tk=tk),
            grid=(B, ATT_KV_HEADS, S // tq),
            in_specs=[
                q_spec,
                pl.BlockSpec((None, S, ATT_HD), lambda b, kh, i: (b, 0, kh)),
                pl.BlockSpec((None, ATT_HD, S), lambda b, kh, i: (b, kh, 0)),
                q_spec,
            ],
            out_specs=q_spec,
            out_shape=jax.ShapeDtypeStruct((B, S, B_WIDTH), BF16),
            scratch_shapes=scratch,
            compiler_params=pltpu.CompilerParams(
                dimension_semantics=("parallel", "parallel", "arbitrary"), vmem_limit_bytes=VMEM_LIMIT),
            name=name)

    bounded = call(_attn_bounded_kernel, [
        pltpu.VMEM((ATT_HD, n), BF16), pltpu.VMEM((2, tk, n), BF16),
        pltpu.VMEM((ATT_HD, n), F32), pltpu.VMEM((8, n), F32)], "attn_bounded")
    online = call(_attn_online_kernel, [
        pltpu.VMEM((ATT_HD, n), BF16), pltpu.VMEM((2, tk, n), F32), pltpu.VMEM((2, 1, n), F32),
        pltpu.VMEM((ATT_HD, n), F32), pltpu.VMEM((8, n), F32), pltpu.VMEM((1, n), F32)], "attn_online")
    return lax.cond(score_bound < ATT_BOUND_LIMIT, bounded, online, qb, kb, vt, szb)


def _ev_out_kernel(x_ref, ma_ref, mb_ref, w_ref, o_ref):
    acc = jnp.dot(ma_ref[...], w_ref[:A_WIDTH, :], preferred_element_type=F32)
    acc = acc + jnp.dot(mb_ref[...], w_ref[A_WIDTH:, :], preferred_element_type=F32)
    o_ref[...] = x_ref[...] + acc


def _ev_out(x, ma, mb, w, *, tm):
    B, S, D = x.shape
    row = lambda width: pl.BlockSpec((None, tm, width), lambda b, i: (b, i, 0))
    return pl.pallas_call(
        _ev_out_kernel,
        grid=(B, S // tm),
        in_specs=[row(D), row(A_WIDTH), row(B_WIDTH), _const_spec((A_WIDTH + B_WIDTH, D))],
        out_specs=row(D),
        out_shape=jax.ShapeDtypeStruct((B, S, D), F32),
        compiler_params=pltpu.CompilerParams(
            dimension_semantics=("parallel", "parallel"), vmem_limit_bytes=VMEM_LIMIT),
        name="ev_out",
    )(x, ma, mb, w)


def _odd_kernel(*refs, seq_len, final):
    if final:
        xp_ref, x_ref, xn_ref, ng_ref, win_ref, pw_ref, pb_ref, ps_ref, wout_ref, fg_ref, o_ref = refs
    else:
        xp_ref, x_ref, xn_ref, ng_ref, win_ref, pw_ref, pb_ref, ps_ref, wout_ref, o_ref = refs
    tm = x_ref.shape[0]
    H = POOL_HALO
    t0 = pl.program_id(1) * tm
    x = x_ref[...]
    xe = jnp.concatenate([x, xp_ref[...], xn_ref[...]], axis=0)
    h = _rms(xe, ng_ref[...]).astype(BF16)
    ue = jnp.dot(h, win_ref[:, :POOL_WIDTH], preferred_element_type=F32)
    z = jnp.dot(h[:tm], win_ref[:, POOL_WIDTH:], preferred_element_type=F32)
    ue_bf = ue.astype(BF16)

    n_sub = tm // POOL_SUB
    pooled = [[None] * n_sub for _ in range(POOL_GROUPS)]
    inv_cnt = [[None] * n_sub for _ in range(POOL_GROUPS)]
    c = lax.broadcasted_iota(jnp.int32, (1, POOL_KWIN + 2 * H), 1)
    for sb in range(n_sub):
        k0 = min(max(sb * POOL_SUB - 2 * H, 0), tm - POOL_KWIN)
        t = t0 + sb * POOL_SUB + lax.broadcasted_iota(jnp.int32, (POOL_SUB, 1), 0)
        pos = t0 + jnp.where(c < POOL_KWIN, k0 + c,
                             jnp.where(c < POOL_KWIN + H, c - POOL_KWIN - H, c - POOL_KWIN - H + tm))
        valid = (pos >= 0) & (pos < seq_len)
        keys = jnp.concatenate([ue_bf[k0:k0 + POOL_KWIN], ue_bf[tm:]], axis=0)
        for g, w in enumerate(POOL_WINDOWS):
            lo = t - w // 2
            hi = lo + w
            band = ((pos >= lo) & (pos < hi) & valid).astype(BF16)
            inv_cnt[g][sb] = 1.0 / (jnp.minimum(hi, seq_len) - jnp.maximum(lo, 0)).astype(F32)
            pooled[g][sb] = jnp.dot(band, keys[:, g * POOL_GW:(g + 1) * POOL_GW], preferred_element_type=F32)
    pooled = [jnp.concatenate(pg, axis=0) for pg in pooled]
    inv_cnt = [jnp.concatenate(ig, axis=0) for ig in inv_cnt]
    ms = []
    for g in range(POOL_GROUPS):
        d = (pooled[g] * inv_cnt[g] - ue[:tm, g * POOL_GW:(g + 1) * POOL_GW]).astype(BF16)
        ms.append(jnp.dot(d, pw_ref[g], preferred_element_type=F32))
    m = jnp.concatenate(ms, axis=1) + pb_ref[...]
    m = (m * ps_ref[...] * _silu(z)).astype(BF16)
    y = x + jnp.dot(m, wout_ref[...], preferred_element_type=F32)
    if final:
        y = _rms(y, fg_ref[...])
    o_ref[...] = y


def _odd(x, ng, win, pw, pb, ps, wout, fg, *, tm):
    B, S, D = x.shape
    H = POOL_HALO
    nh = tm // H
    last = S // H - 1
    row = pl.BlockSpec((None, tm, D), lambda b, i: (b, i, 0))
    prev = pl.BlockSpec((None, H, D), lambda b, i: (b, jnp.maximum(i * nh - 1, 0), 0))
    nxt = pl.BlockSpec((None, H, D), lambda b, i: (b, jnp.minimum((i + 1) * nh, last), 0))
    final = fg is not None
    in_specs = [
        prev, row, nxt, _const_spec((1, D)), _const_spec((D, 2 * POOL_WIDTH)),
        _const_spec((POOL_GROUPS, POOL_GW, POOL_GW)), _const_spec((1, POOL_WIDTH)),
        _const_spec((1, POOL_WIDTH)), _const_spec((POOL_WIDTH, D)),
    ]
    args = [x, x, x, ng, win, pw, pb, ps, wout]
    if final:
        in_specs.append(_const_spec((1, D)))
        args.append(fg)
    return pl.pallas_call(
        functools.partial(_odd_kernel, seq_len=S, final=final),
        grid=(B, S // tm),
        in_specs=in_specs,
        out_specs=row,
        out_shape=jax.ShapeDtypeStruct((B, S, D), F32),
        compiler_params=pltpu.CompilerParams(
            dimension_semantics=("parallel", "parallel"), vmem_limit_bytes=VMEM_LIMIT),
        name="odd_final" if final else "odd",
    )(*args)


def _rope_tables(S):
    rows = S // GRID_W
    row = jnp.repeat(jnp.arange(rows, dtype=F32), GRID_W)
    col = jnp.tile(jnp.arange(GRID_W, dtype=F32), rows)
    n_pairs = ATT_HD // 4
    freqs = ROPE_THETA ** (-jnp.arange(n_pairs, dtype=F32) / n_pairs)
    ang = jnp.concatenate([row[:, None] * freqs, col[:, None] * freqs], axis=-1)
    c, s = jnp.cos(ang), jnp.sin(ang)
    return jnp.concatenate([c, c], axis=-1), jnp.concatenate([-s, s], axis=-1)


_PAIR_PERM = np.concatenate([np.arange(0, ATT_HD, 2), np.arange(1, ATT_HD, 2)])


def _even_weights(w_in, gate_w, gate_b, q_norm_g, k_norm_g):
    offs = np.concatenate([[0], np.cumsum(EV_SPLITS)])
    w_in = w_in.astype(BF16)
    seg = lambda i: w_in[:, offs[i]:offs[i + 1]]
    q_a, k_a, v_a, lr_f, lr_b, z_a, q_b, k_b, v_b, z_b = [seg(i) for i in range(10)]
    D = w_in.shape[0]

    def perm_heads(w, nh):
        return w.reshape(D, nh, ATT_HD)[:, :, _PAIR_PERM].reshape(D, nh * ATT_HD)

    lr = jnp.concatenate([lr_f, lr_b, jnp.zeros((D, LR_PAD - 2 * GLA_LOWRANK), w_in.dtype)], axis=1)
    w = jnp.concatenate(
        [q_a, k_a, v_a, z_a, perm_heads(q_b, ATT_HEADS), perm_heads(k_b, ATT_KV_HEADS), v_b, z_b, lr], axis=1)
    gw = jnp.zeros((LR_PAD, 2 * QK_A), F32)
    gw = gw.at[:GLA_LOWRANK, :QK_A].set(gate_w[0])
    gw = gw.at[GLA_LOWRANK:2 * GLA_LOWRANK, QK_A:].set(gate_w[1])
    gb = jnp.concatenate([gate_b[0], gate_b[1]])[None, :]
    return (w, gw.astype(BF16), gb.astype(F32),
            q_norm_g[_PAIR_PERM][None, :], k_norm_g[_PAIR_PERM][None, :])


def _prepare(norm_g, final_norm_g, ev_w_in, ev_gla_gate_w, ev_gla_gate_b, ev_gla_norm_g,
             ev_q_norm_g, ev_k_norm_g, ev_w_out, od_w_in, od_pool_w, od_pool_b, od_pool_scale, od_w_out):
    layers = []
    for layer in range(DEPTH):
        i = layer // 2
        ng = norm_g[layer][None, :]
        if layer % 2 == 0:
            bound = (ATT_HD * (ATT_HD ** -0.5) * LOG2E * 1.02
                     * jnp.max(jnp.abs(ev_q_norm_g[i])) * jnp.max(jnp.abs(ev_k_norm_g[i])))
            layers.append((ng, _even_weights(ev_w_in[i], ev_gla_gate_w[i], ev_gla_gate_b[i],
                                             ev_q_norm_g[i], ev_k_norm_g[i]),
                           ev_gla_norm_g[i][None, :], bound, ev_w_out[i].astype(BF16)))
        else:
            fg = final_norm_g[None, :] if layer == DEPTH - 1 else None
            layers.append((ng, od_w_in[i].astype(BF16), od_pool_w[i].astype(BF16), od_pool_b[i][None, :],
                           od_pool_scale[i][None, :], od_w_out[i].astype(BF16), fg))
    return layers


def _run(x, layers, rope):
    B, S, D = x.shape
    cos2, sin2 = rope
    tm_in = min(512, S)
    tm_out = min(1024, S)
    tm_odd = min(512, S)
    gla_rows = min(1024, S)
    tq = min(2048, S)
    tk = min(256, S)
    for layer, ops in enumerate(layers):
        if layer % 2 == 0:
            ng, (w, gw, gb, qg, kg), gla_g, bound, w_out = ops
            qa, ka, va, la, sza, qb, kb, vt, szb = _ev_in(x, ng, w, gw, gb, qg, kg, cos2, sin2, tm=tm_in)
            o_bwd = _gla(qa, ka, va, la, reverse=True, rows=gla_rows)
            ma = _gla(qa, ka, va, la, reverse=False, rows=gla_rows, extra=(o_bwd, sza, gla_g))
            mb = _attention(qb, kb, vt, szb, bound, tq=tq, tk=tk)
            x = _ev_out(x, ma, mb, w_out, tm=tm_out)
        else:
            ng, win, pw, pb, ps, wout, fg = ops
            x = _odd(x, ng, win, pw, pb, ps, wout, fg, tm=tm_odd)
    return x


def _trunk(x, *weights):
    return _run(x, _prepare(*weights), _rope_tables(x.shape[1]))


def kernel(x_prompt, x_sample, norm_g, final_norm_g, ev_w_in, ev_gla_gate_w, ev_gla_gate_b,
           ev_gla_norm_g, ev_q_norm_g, ev_k_norm_g, ev_w_out, od_w_in, od_pool_w, od_pool_b,
           od_pool_scale, od_w_out):
    layers = _prepare(norm_g, final_norm_g, ev_w_in, ev_gla_gate_w, ev_gla_gate_b, ev_gla_norm_g,
                      ev_q_norm_g, ev_k_norm_g, ev_w_out, od_w_in, od_pool_w, od_pool_b, od_pool_scale,
                      od_w_out)
    rope = _rope_tables(x_prompt.shape[1])
    rope_s = rope if x_sample.shape[1] == x_prompt.shape[1] else _rope_tables(x_sample.shape[1])
    return (_run(x_prompt, layers, rope), _run(x_sample, layers, rope_s))
```
